```python
import math
import jax, jax.numpy as jnp
from jax import lax
import numpy as np

D_MODEL = 1024
BATCH = 4
SEQ = 4096
DEPTH = 4

N_MIXERS = 3
N_MLA = (DEPTH + 2) // 3
N_DIFF = (DEPTH + 1) // 3
N_LRU = DEPTH // 3
N_META = 16
Q_BLOCK = 128
NORM_EPS = 1e-6
MLA_HEADS = 16
MLA_Q_RANK = 384
MLA_KV_RANK = 256
MLA_NOPE = 128
MLA_ROPE = 64
MLA_V = 128
MLA_IN_WIDTH = MLA_Q_RANK + MLA_KV_RANK + MLA_ROPE
ROPE_THETA = 10000.0
DIFF_HEAD_DIM = 64
DIFF_HEADS = D_MODEL // (2 * DIFF_HEAD_DIM)
DIFF_WIDTH = DIFF_HEADS * 2 * DIFF_HEAD_DIM
LRU_WIDTH = 1536
LRU_BLOCKS = 6
LRU_BLOCK_W = LRU_WIDTH // LRU_BLOCKS
CONV_WIDTH = 4
CONV_PAD = (2, 1)
LRU_C = 8.0
FFN_HIDDEN = -(-(8 * D_MODEL) // (3 * 256)) * 256

kernel_name = 'hybrid_mla_diffattn_rglru_encoder'


def _rms_norm(x, g):
    xf = x.astype(jnp.float32)
    y = xf * lax.rsqrt(jnp.mean(xf * xf, axis=-1, keepdims=True) + NORM_EPS)
    return (y * g.astype(jnp.float32)).astype(x.dtype)


def _rope_tables(n_pos):
    pos = jnp.arange(n_pos, dtype=jnp.float32)
    inv_freq = ROPE_THETA ** (-jnp.arange(0, MLA_ROPE, 2, dtype=jnp.float32) / MLA_ROPE)
    ang = pos[:, None] * inv_freq[None, :]
    return jnp.cos(ang), jnp.sin(ang)


def _rope(x, cos, sin):
    xf = x.astype(jnp.float32)
    x1, x2 = jnp.split(xf, 2, axis=-1)
    return jnp.concatenate([x1 * cos - x2 * sin, x2 * cos + x1 * sin], axis=-1).astype(x.dtype)


def _sweep_query_blocks(block_fn, q_arrays):
    b, n_pos = q_arrays[0].shape[:2]
    n_real = n_pos - N_META
    nb = n_real // Q_BLOCK
    meta_out = block_fn(tuple(a[:, :N_META] for a in q_arrays), jnp.arange(N_META))
    blocks = tuple(jnp.moveaxis(a[:, N_META:].reshape(b, nb, Q_BLOCK, *a.shape[2:]), 1, 0) for a in q_arrays)
    pos_blocks = (N_META + jnp.arange(n_real)).reshape(nb, Q_BLOCK)
    outs = lax.map(lambda args: block_fn(args[0], args[1]), (blocks, pos_blocks))
    outs = jnp.moveaxis(outs, 0, 1)
    outs = outs.reshape(b, n_real, *outs.shape[3:])
    return jnp.concatenate([meta_out.astype(outs.dtype), outs], axis=1)


def _mla(h, w_in, q_norm_g, kv_norm_g, w_uq, w_ukv, w_o, cos, sin):
    b, n_pos, _ = h.shape
    c_q, c_kv, k_rope = jnp.split(h @ w_in, [MLA_Q_RANK, MLA_Q_RANK + MLA_KV_RANK], axis=-1)
    q = (_rms_norm(c_q, q_norm_g) @ w_uq).reshape(b, n_pos, MLA_HEADS, MLA_NOPE + MLA_ROPE)
    q_nope = q[..., :MLA_NOPE]
    q_rope = _rope(q[..., MLA_NOPE:], cos[:, None, :], sin[:, None, :])
    kv = (_rms_norm(c_kv, kv_norm_g) @ w_ukv).reshape(b, n_pos, MLA_HEADS, MLA_NOPE + MLA_V)
    k_nope, v = kv[..., :MLA_NOPE], kv[..., MLA_NOPE:].astype(jnp.float32)
    k_rope = _rope(k_rope, cos, sin)
    scale = (MLA_NOPE + MLA_ROPE) ** -0.5

    def block(qs, _qpos):
        qn, qr = qs
        s = (jnp.einsum('bqhd,bkhd->bhqk', qn, k_nope, preferred_element_type=jnp.float32)
             + jnp.einsum('bqhr,bkr->bhqk', qr, k_rope, preferred_element_type=jnp.float32))
        p = jax.nn.softmax(s * scale, axis=-1)
        return jnp.einsum('bhqk,bkhd->bqhd', p, v)

    o = _sweep_query_blocks(block, (q_nope, q_rope))
    return o.reshape(b, n_pos, MLA_HEADS * MLA_V).astype(h.dtype) @ w_o


def _diff_attn(h, w_in, lam_p, subln_g, w_o, lambda_init):
    b, n_pos, _ = h.shape
    q, k, v = jnp.split(h @ w_in, 3, axis=-1)
    q = q.reshape(b, n_pos, DIFF_HEADS, 2, DIFF_HEAD_DIM)
    k = k.reshape(b, n_pos, DIFF_HEADS, 2, DIFF_HEAD_DIM)
    v = v.reshape(b, n_pos, DIFF_HEADS, 2 * DIFF_HEAD_DIM).astype(jnp.float32)
    lp = lam_p.astype(jnp.float32)
    lam = jnp.exp(jnp.sum(lp[0] * lp[1])) - jnp.exp(jnp.sum(lp[2] * lp[3])) + lambda_init
    slopes = 2.0 ** (-8.0 * jnp.arange(1, DIFF_HEADS + 1, dtype=jnp.float32) / DIFF_HEADS)
    k_pos = jnp.arange(n_pos)
    scale = DIFF_HEAD_DIM ** -0.5

    def block(qs, q_pos):
        (qb,) = qs
        s = jnp.einsum('bqhcd,bkhcd->bchqk', qb, k, preferred_element_type=jnp.float32) * scale
        dist = jnp.abs(q_pos[:, None] - k_pos[None, :]).astype(jnp.float32)
        s = s - slopes[:, None, None] * dist
        p = jax.nn.softmax(s, axis=-1)
        p = p[:, 0] - lam * p[:, 1]
        return jnp.einsum('bhqk,bkhd->bqhd', p, v)

    o = _sweep_query_blocks(block, (q,))
    o = _rms_norm(o, subln_g) * (1.0 - lambda_init)
    return o.reshape(b, n_pos, DIFF_WIDTH).astype(h.dtype) @ w_o


def _scan_combine(e1, e2):
    a1, b1 = e1
    a2, b2 = e2
    return a1 * a2, a2 * b1 + b2


def _rg_lru(x, w_g, b_g, lam, reverse):
    b, n_pos, _ = x.shape
    xf = x.astype(jnp.float32)
    xblk = xf.reshape(b, n_pos, LRU_BLOCKS, LRU_BLOCK_W)
    gates = jnp.einsum('blnc,gncd->gblnd', xblk, w_g.astype(jnp.float32)).reshape(2, b, n_pos, LRU_WIDTH)
    gates = jax.nn.sigmoid(gates + b_g.astype(jnp.float32)[:, None, None, :])
    r, i = gates[0], gates[1]
    log_a = -LRU_C * r * jax.nn.softplus(-lam.astype(jnp.float32))
    a = jnp.exp(log_a)
    u = jnp.sqrt(-jnp.expm1(2.0 * log_a)) * (i * xf)
    _, hs = lax.associative_scan(_scan_combine, (a, u), axis=1, reverse=reverse)
    return hs


def _rglru_block(h, w_in, conv_w, conv_b, w_gates, b_gates, lam, w_o):
    xb, gate = jnp.split(h @ w_in, 2, axis=-1)
    xb = lax.conv_general_dilated(xb, conv_w[:, None, :].astype(xb.dtype), window_strides=(1,),
                                  padding=[CONV_PAD], dimension_numbers=('NWC', 'WIO', 'NWC'),
                                  feature_group_count=LRU_WIDTH) + conv_b
    y = (_rg_lru(xb, w_gates[0], b_gates[0], lam[0], reverse=False)
         + _rg_lru(xb, w_gates[1], b_gates[1], lam[1], reverse=True))
    return (y.astype(h.dtype) * jax.nn.gelu(gate)) @ w_o


def _swiglu(h, w_in, w_out):
    g, u = jnp.split(h @ w_in, 2, axis=-1)
    return (jax.nn.silu(g) * u) @ w_out


def setup_inputs(seed: int = 0) -> dict:
    key = jax.random.key(seed)
    ks = iter(jax.random.split(key, 24))

    def nrm(shape, scale):
        return scale * jax.random.normal(next(ks), shape, jnp.float32)

    D = D_MODEL
    x = nrm((BATCH, SEQ, D), 1.0)
    meta_tokens = nrm((N_META, D), 1.0)
    norm_g = 1.0 + nrm((DEPTH, 4, D), 0.05)
    mla_w_in = nrm((N_MLA, D, MLA_IN_WIDTH), D ** -0.5)
    mla_q_norm = 1.0 + nrm((N_MLA, MLA_Q_RANK), 0.05)
    mla_kv_norm = 1.0 + nrm((N_MLA, MLA_KV_RANK), 0.05)
    mla_w_uq = nrm((N_MLA, MLA_Q_RANK, MLA_HEADS * (MLA_NOPE + MLA_ROPE)), MLA_Q_RANK ** -0.5)
    mla_w_ukv = nrm((N_MLA, MLA_KV_RANK, MLA_HEADS * (MLA_NOPE + MLA_V)), MLA_KV_RANK ** -0.5)
    mla_w_o = nrm((N_MLA, MLA_HEADS * MLA_V, D), (MLA_HEADS * MLA_V) ** -0.5)
    diff_w_in = nrm((N_DIFF, D, 3 * DIFF_WIDTH), D ** -0.5)
    diff_lambda = nrm((N_DIFF, 4, DIFF_HEAD_DIM), 0.1)
    diff_subln = 1.0 + nrm((N_DIFF, 2 * DIFF_HEAD_DIM), 0.05)
    diff_w_o = nrm((N_DIFF, DIFF_WIDTH, D), DIFF_WIDTH ** -0.5)
    lru_w_in = nrm((N_LRU, D, 2 * LRU_WIDTH), D ** -0.5)
    lru_conv_w = nrm((N_LRU, CONV_WIDTH, LRU_WIDTH), CONV_WIDTH ** -0.5)
    lru_conv_b = nrm((N_LRU, LRU_WIDTH), 0.01)
    lru_w_gates = nrm((N_LRU, 2, 2, LRU_BLOCKS, LRU_BLOCK_W, LRU_BLOCK_W), LRU_BLOCK_W ** -0.5)
    lru_b_gates = nrm((N_LRU, 2, 2, LRU_WIDTH), 0.01)
    a_c = jax.random.uniform(next(ks), (N_LRU, 2, LRU_WIDTH), jnp.float32, 0.9, 0.999)
    s = a_c ** (1.0 / LRU_C)
    lru_lambda = jnp.log(s) - jnp.log1p(-s)
    lru_w_o = nrm((N_LRU, LRU_WIDTH, D), LRU_WIDTH ** -0.5)
    ffn_w_in = nrm((DEPTH, D, 2 * FFN_HIDDEN), D ** -0.5)
    ffn_w_out = nrm((DEPTH, FFN_HIDDEN, D), FFN_HIDDEN ** -0.5)
    return {'x': x, 'meta_tokens': meta_tokens, 'norm_g': norm_g,
            'mla_w_in': mla_w_in, 'mla_q_norm': mla_q_norm, 'mla_kv_norm': mla_kv_norm,
            'mla_w_uq': mla_w_uq, 'mla_w_ukv': mla_w_ukv, 'mla_w_o': mla_w_o,
            'diff_w_in': diff_w_in, 'diff_lambda': diff_lambda, 'diff_subln': diff_subln, 'diff_w_o': diff_w_o,
            'lru_w_in': lru_w_in, 'lru_conv_w': lru_conv_w, 'lru_conv_b': lru_conv_b,
            'lru_w_gates': lru_w_gates, 'lru_b_gates': lru_b_gates, 'lru_lambda': lru_lambda, 'lru_w_o': lru_w_o,
            'ffn_w_in': ffn_w_in, 'ffn_w_out': ffn_w_out}


def reference(x, meta_tokens, norm_g,
              mla_w_in, mla_q_norm, mla_kv_norm, mla_w_uq, mla_w_ukv, mla_w_o,
              diff_w_in, diff_lambda, diff_subln, diff_w_o,
              lru_w_in, lru_conv_w, lru_conv_b, lru_w_gates, lru_b_gates, lru_lambda, lru_w_o,
              ffn_w_in, ffn_w_out):
    b = x.shape[0]
    meta = jnp.broadcast_to(meta_tokens[None].astype(x.dtype), (b, N_META, D_MODEL))
    h = jnp.concatenate([meta, x], axis=1)
    cos, sin = _rope_tables(h.shape[1])
    for i in range(DEPTH):
        kind, j = i % N_MIXERS, i // N_MIXERS
        g = norm_g[i]
        u = _rms_norm(h, g[0])
        if kind == 0:
            u = _mla(u, mla_w_in[j], mla_q_norm[j], mla_kv_norm[j], mla_w_uq[j], mla_w_ukv[j], mla_w_o[j], cos, sin)
        elif kind == 1:
            lambda_init = 0.8 - 0.6 * math.exp(-0.3 * i)
            u = _diff_attn(u, diff_w_in[j], diff_lambda[j], diff_subln[j], diff_w_o[j], lambda_init)
        else:
            u = _rglru_block(u, lru_w_in[j], lru_conv_w[j], lru_conv_b[j], lru_w_gates[j], lru_b_gates[j],
                             lru_lambda[j], lru_w_o[j])
        h = h + _rms_norm(u, g[1])
        u = _swiglu(_rms_norm(h, g[2]), ffn_w_in[i], ffn_w_out[i])
        h = h + _rms_norm(u, g[3])
    return h[:, N_META:]
```

```python
import functools
import math

import jax
import jax.numpy as jnp
import numpy as np
from jax import lax
from jax.experimental import pallas as pl
from jax.experimental.pallas import tpu as pltpu

F32 = jnp.float32
BF16 = jnp.bfloat16

D_MODEL = 1024
N_META = 16
ROW_OFF = 128
N_PAD = ROW_OFF - N_META
NORM_EPS = 1e-6
LOG2E = math.log2(math.e)
NEG_BIG = -1e30

MLA_HEADS = 16
MLA_Q_RANK = 384
MLA_KV_RANK = 256
MLA_NOPE = 128
MLA_ROPE = 64
MLA_V = 128
MLA_QK = 256
ROPE_THETA = 10000.0

DIFF_HEADS = 8
DIFF_HEAD_DIM = 64

LRU_WIDTH = 1536
LRU_BLOCKS = 6
LRU_BLOCK_W = 256
LRU_C = 8.0
LRU_CHUNK = 384

FFN_HIDDEN = 2816
FFN_CHUNK = 1408

ROW_TILE = 384
VMEM_LIMIT = 56 * 1024 * 1024


def _cparams(n_axes):
    return pltpu.CompilerParams(dimension_semantics=("arbitrary",) * n_axes,
                                vmem_limit_bytes=VMEM_LIMIT)


def _rms(x, g):
    return x * lax.rsqrt(jnp.mean(x * x, axis=-1, keepdims=True) + NORM_EPS) * g


def _dot(a, b):
    return jnp.dot(a, b, preferred_element_type=F32)


def _dot_nt(a, b):
    return lax.dot_general(a, b, (((1,), (1,)), ((), ())), preferred_element_type=F32)


def _full(shape):
    nd = len(shape)
    return pl.BlockSpec(shape, lambda *_: (0,) * nd)


def _rope128(v, cos, sin_lo, sin_hi):
    return v * cos + pltpu.roll(v, 96, 1) * sin_lo + pltpu.roll(v, 32, 1) * sin_hi


def _mla_prep_kernel(h_ref, g_ref, win_ref, qg_ref, kvg_ref, wuq_ref, wuk_ref, wuv_ref,
                     cos_ref, slo_ref, shi_ref, q_ref, k_ref, v_ref, *, q_scale):
    u = _rms(h_ref[...], g_ref[...]).astype(BF16)
    t = _dot(u, win_ref[...])
    cq = _rms(t[:, :MLA_Q_RANK], qg_ref[...]).astype(BF16)
    ckv = _rms(t[:, MLA_Q_RANK:MLA_Q_RANK + MLA_KV_RANK], kvg_ref[...]).astype(BF16)
    cos, slo, shi = cos_ref[...], slo_ref[...], shi_ref[...]
    kr = _rope128(t[:, MLA_Q_RANK + MLA_KV_RANK:], cos, slo, shi).astype(BF16)
    for hp in range(MLA_HEADS // 2):
        kn = _dot(ckv, wuk_ref[:, hp * 256:(hp + 1) * 256])
        v_ref[:, hp * 256:(hp + 1) * 256] = _dot(ckv, wuv_ref[:, hp * 256:(hp + 1) * 256]).astype(BF16)
        for j in range(2):
            h = 2 * hp + j
            q = _dot(cq, wuq_ref[:, h * MLA_QK:(h + 1) * MLA_QK])
            q_ref[:, h * MLA_QK:h * MLA_QK + 128] = (q[:, :128] * q_scale).astype(BF16)
            q_ref[:, h * MLA_QK + 128:(h + 1) * MLA_QK] = (
                _rope128(q[:, 128:], cos, slo, shi) * q_scale).astype(BF16)
            k_ref[:, h * MLA_QK:h * MLA_QK + 128] = kn[:, j * 128:(j + 1) * 128].astype(BF16)
            k_ref[:, h * MLA_QK + 128:(h + 1) * MLA_QK] = kr


def _mla_prep(h, g, w_in, qg, kvg, wuq, wuk, wuv, cos, slo, shi, lp):
    rows = h.shape[0]
    tm = ROW_TILE
    tiles_per_seq = lp // tm
    row = lambda i: (i, 0)
    pos = lambda i: (i % tiles_per_seq, 0)
    q_scale = (MLA_NOPE + MLA_ROPE) ** -0.5 * LOG2E
    return pl.pallas_call(
        functools.partial(_mla_prep_kernel, q_scale=q_scale),
        grid=(rows // tm,),
        in_specs=[pl.BlockSpec((tm, D_MODEL), row), _full(g.shape), _full(w_in.shape),
                  _full(qg.shape), _full(kvg.shape), _full(wuq.shape), _full(wuk.shape),
                  _full(wuv.shape), pl.BlockSpec((tm, 128), pos), pl.BlockSpec((tm, 128), pos),
                  pl.BlockSpec((tm, 128), pos)],
        out_specs=[pl.BlockSpec((tm, MLA_HEADS * MLA_QK), row),
                   pl.BlockSpec((tm, MLA_HEADS * MLA_QK), row),
                   pl.BlockSpec((tm, MLA_HEADS * MLA_V), row)],
        out_shape=[jax.ShapeDtypeStruct((rows, MLA_HEADS * MLA_QK), BF16),
                   jax.ShapeDtypeStruct((rows, MLA_HEADS * MLA_QK), BF16),
                   jax.ShapeDtypeStruct((rows, MLA_HEADS * MLA_V), BF16)],
        compiler_params=_cparams(1),
        name="mla_prep",
    )(h, g, w_in, qg, kvg, wuq, wuk, wuv, cos, slo, shi)


def _mla_attn_kernel(q_ref, k_ref, v_ref, o_ref, *, seq, tq, tk):
    def q_tile(r0, rows):
        q = q_ref[pl.ds(r0, rows), :]
        s = _dot_nt(q, k_ref[0:ROW_OFF, :])
        col = lax.broadcasted_iota(jnp.int32, (rows, ROW_OFF), 1)
        s = jnp.where(col >= N_PAD, s, NEG_BIG)
        m = jnp.max(s, axis=1, keepdims=True)
        p = jnp.exp2(s - m)
        l = jnp.sum(p, axis=1, keepdims=True)
        acc = _dot(p.astype(BF16), v_ref[0:ROW_OFF, :])
        for c in range(seq // tk):
            k0 = ROW_OFF + c * tk
            s = _dot_nt(q, k_ref[k0:k0 + tk, :])
            m_new = jnp.maximum(m, jnp.max(s, axis=1, keepdims=True))
            alpha = jnp.exp2(m - m_new)
            p = jnp.exp2(s - m_new)
            l = alpha * l + jnp.sum(p, axis=1, keepdims=True)
            acc = alpha * acc + _dot(p.astype(BF16), v_ref[k0:k0 + tk, :])
            m = m_new
        o_ref[pl.ds(r0, rows), :] = (acc / l).astype(BF16)

    q_tile(0, ROW_OFF)

    def body(i, carry):
        q_tile(pl.multiple_of(ROW_OFF + i * tq, 128), tq)
        return carry

    lax.fori_loop(0, seq // tq, body, 0)


def _mla_attn(q, k, v, seq):
    b, lp, _ = q.shape
    tq = min(256, seq)
    tk = min(1024, seq)
    return pl.pallas_call(
        functools.partial(_mla_attn_kernel, seq=seq, tq=tq, tk=tk),
        grid=(b, MLA_HEADS),
        in_specs=[pl.BlockSpec((None, lp, MLA_QK), lambda i, j: (i, 0, j)),
                  pl.BlockSpec((None, lp, MLA_QK), lambda i, j: (i, 0, j)),
                  pl.BlockSpec((None, lp, MLA_V), lambda i, j: (i, 0, j))],
        out_specs=pl.BlockSpec((None, lp, MLA_V), lambda i, j: (i, 0, j)),
        out_shape=jax.ShapeDtypeStruct((b, lp, MLA_HEADS * MLA_V), BF16),
        compiler_params=_cparams(2),
        name="mla_attn",
    )(q, k, v)


def _diff_prep_kernel(h_ref, g_ref, w_ref, q_ref, k_ref, v_ref, *, q_scale):
    u = _rms(h_ref[...], g_ref[...]).astype(BF16)
    n = q_ref.shape[1]
    for c in range(n // 256):
        sl = slice(c * 256, (c + 1) * 256)
        q_ref[:, sl] = (_dot(u, w_ref[:, c * 256:(c + 1) * 256]) * q_scale).astype(BF16)
        k_ref[:, sl] = _dot(u, w_ref[:, n + c * 256:n + (c + 1) * 256]).astype(BF16)
        v_ref[:, sl] = _dot(u, w_ref[:, 2 * n + c * 256:2 * n + (c + 1) * 256]).astype(BF16)


def _diff_prep(h, g, w):
    rows = h.shape[0]
    tm = ROW_TILE
    n = w.shape[1] // 3
    row = lambda i: (i, 0)
    return pl.pallas_call(
        functools.partial(_diff_prep_kernel, q_scale=DIFF_HEAD_DIM ** -0.5 * LOG2E),
        grid=(rows // tm,),
        in_specs=[pl.BlockSpec((tm, D_MODEL), row), _full(g.shape), _full(w.shape)],
        out_specs=[pl.BlockSpec((tm, n), row)] * 3,
        out_shape=[jax.ShapeDtypeStruct((rows, n), BF16)] * 3,
        compiler_params=_cparams(1),
        name="diff_prep",
    )(h, g, w)


def _diff_attn_kernel(slope_ref, q_ref, k_ref, v_ref, lam_ref, g_ref, o_ref, *,
                      seq, tq, tk, lambda_init):
    slope = slope_ref[pl.program_id(1)] * LOG2E
    lp4 = lam_ref[...]
    lam = (jnp.exp(jnp.sum(lp4[0:1] * lp4[1:2], axis=1, keepdims=True))
           - jnp.exp(jnp.sum(lp4[2:3] * lp4[3:4], axis=1, keepdims=True)) + lambda_init)
    g = g_ref[...]

    def q_tile(r0, rows):
        q = q_ref[pl.ds(r0, rows), :]
        lane = lax.broadcasted_iota(jnp.int32, q.shape, 1)
        zero = jnp.zeros_like(q)
        qs = (jnp.where(lane < DIFF_HEAD_DIM, q, zero), jnp.where(lane >= DIFF_HEAD_DIM, q, zero))

        def scores(k0, width, mask_pad):
            kk = k_ref[k0:k0 + width, :]
            d = (lax.broadcasted_iota(jnp.int32, (rows, width), 0)
                 - lax.broadcasted_iota(jnp.int32, (rows, width), 1)).astype(F32)
            off = r0 - k0
            off = float(off) if isinstance(off, int) else off.astype(F32)
            bias = jnp.abs(d + off) * slope
            out = []
            for c in range(2):
                s = _dot_nt(qs[c], kk) - bias
                if mask_pad:
                    col = lax.broadcasted_iota(jnp.int32, (rows, width), 1)
                    s = jnp.where(col >= N_PAD, s, NEG_BIG)
                out.append(s)
            return out

        s2 = scores(0, ROW_OFF, True)
        vv = v_ref[0:ROW_OFF, :]
        m, l, acc = [], [], []
        for c in range(2):
            mc = jnp.max(s2[c], axis=1, keepdims=True)
            p = jnp.exp2(s2[c] - mc)
            m.append(mc)
            l.append(jnp.sum(p, axis=1, keepdims=True))
            acc.append(_dot(p.astype(BF16), vv))
        for ch in range(seq // tk):
            k0 = ROW_OFF + ch * tk
            s2 = scores(k0, tk, False)
            vv = v_ref[k0:k0 + tk, :]
            for c in range(2):
                m_new = jnp.maximum(m[c], jnp.max(s2[c], axis=1, keepdims=True))
                alpha = jnp.exp2(m[c] - m_new)
                p = jnp.exp2(s2[c] - m_new)
                l[c] = alpha * l[c] + jnp.sum(p, axis=1, keepdims=True)
                acc[c] = alpha * acc[c] + _dot(p.astype(BF16), vv)
                m[c] = m_new
        o = acc[0] / l[0] - lam * (acc[1] / l[1])
        o = _rms(o, g) * (1.0 - lambda_init)
        o_ref[pl.ds(r0, rows), :] = o.astype(BF16)

    q_tile(0, ROW_OFF)

    def body(i, carry):
        q_tile(pl.multiple_of(ROW_OFF + i * tq, 128), tq)
        return carry

    lax.fori_loop(0, seq // tq, body, 0)


def _diff_attn(slopes, q, k, v, lam_p, g, seq, lambda_init):
    b, lp, _ = q.shape
    tq = min(256, seq)
    tk = min(512, seq)
    w = 2 * DIFF_HEAD_DIM
    blk = pl.BlockSpec((None, lp, w), lambda i, j: (i, 0, j))
    return pl.pallas_call(
        functools.partial(_diff_attn_kernel, seq=seq, tq=tq, tk=tk, lambda_init=lambda_init),
        grid=(b, DIFF_HEADS),
        in_specs=[pl.BlockSpec(memory_space=pltpu.SMEM), blk, blk, blk,
                  _full(lam_p.shape), _full(g.shape)],
        out_specs=blk,
        out_shape=jax.ShapeDtypeStruct((b, lp, DIFF_HEADS * w), BF16),
        compiler_params=_cparams(2),
        name="diff_attn",
    )(slopes, q, k, v, lam_p, g)


def _gelu_tanh(x):
    c = math.sqrt(2.0 / math.pi)
    return x * (0.5 * (1.0 + jnp.tanh(c * (x + 0.044715 * (x * x * x)))))


def _lru_prep_kernel(h_ref, g_ref, w_ref, xb_ref, gate_ref):
    u = _rms(h_ref[...], g_ref[...]).astype(BF16)
    for c in range(LRU_WIDTH // 256):
        sl = slice(c * 256, (c + 1) * 256)
        xb_ref[:, sl] = _dot(u, w_ref[:, c * 256:(c + 1) * 256])
        gate = _dot(u, w_ref[:, LRU_WIDTH + c * 256:LRU_WIDTH + (c + 1) * 256])
        gate_ref[:, sl] = _gelu_tanh(gate).astype(BF16)


def _lru_prep(h, g, w):
    rows = h.shape[0]
    tm = ROW_TILE
    row = lambda i: (i, 0)
    return pl.pallas_call(
        _lru_prep_kernel,
        grid=(rows // tm,),
        in_specs=[pl.BlockSpec((tm, D_MODEL), row), _full(g.shape), _full(w.shape)],
        out_specs=[pl.BlockSpec((tm, LRU_WIDTH), row)] * 2,
        out_shape=[jax.ShapeDtypeStruct((rows, LRU_WIDTH), F32),
                   jax.ShapeDtypeStruct((rows, LRU_WIDTH), BF16)],
        compiler_params=_cparams(1),
        name="lru_prep",
    )(h, g, w)


def _softplus(x):
    return jnp.maximum(x, 0.0) + jnp.log(1.0 + jnp.exp(-jnp.abs(x)))


def _group_scan(a, u, carry, reverse):
    row = lax.broadcasted_iota(jnp.int32, a.shape, 0)
    for k in (1, 2, 4):
        shift = (8 - k) if reverse else k
        keep = (row < 8 - k) if reverse else (row >= k)
        a_s = pltpu.roll(a, shift, 0)
        u_s = pltpu.roll(u, shift, 0)
        u = jnp.where(keep, a * u_s + u, u)
        a = jnp.where(keep, a * a_s, a)
    return a * carry + u


def _lru_scan_kernel(xb_ref, gate_ref, cw_ref, cb_ref, wg_ref, bg_ref, lam_ref, o_ref,
                     xc_ref, y_ref, *, lp):
    ch = LRU_CHUNK
    n_chunks = lp // ch
    cw = cw_ref[...]
    cb = cb_ref[...]

    for c in range(n_chunks):
        t0 = c * ch
        lo = 8 if c == 0 else 0
        hi = ch - 8 if c == n_chunks - 1 else ch
        n = hi - lo
        acc = cb
        for j in range(4):
            x = xb_ref[t0 + lo + j - 2:t0 + lo + j - 2 + n, :]
            if c == 0:
                r = lax.broadcasted_iota(jnp.int32, x.shape, 0) + (lo + j - 2)
                x = jnp.where(r >= N_PAD, x, 0.0)
            acc = acc + cw[j:j + 1] * x
        xc_ref[t0 + lo:t0 + hi, :] = acc
        if c == 0:
            xc_ref[0:8, :] = jnp.zeros((8, xc_ref.shape[1]), F32)
        if c == n_chunks - 1:
            base = lp - 8
            tail = xb_ref[base:lp, :]
            prev = xb_ref[base - 8:base, :]
            r = lax.broadcasted_iota(jnp.int32, tail.shape, 0)
            x_m2 = jnp.where(r >= 2, pltpu.roll(tail, 2, 0), pltpu.roll(prev, 2, 0))
            x_m1 = jnp.where(r >= 1, pltpu.roll(tail, 1, 0), pltpu.roll(prev, 1, 0))
            x_p1 = jnp.where(r < 7, pltpu.roll(tail, 7, 0), 0.0)
            xc_ref[base:lp, :] = (cb + cw[0:1] * x_m2 + cw[1:2] * x_m1
                                  + cw[2:3] * tail + cw[3:4] * x_p1)

    def gates(t0, d):
        xc = xc_ref[pl.ds(t0, ch), :]
        xcb = xc.astype(BF16)
        r = jax.nn.sigmoid(_dot(xcb, wg_ref[d, 0]) + bg_ref[2 * d:2 * d + 1, :])
        i = jax.nn.sigmoid(_dot(xcb, wg_ref[d, 1]) + bg_ref[2 * d + 1:2 * d + 2, :])
        log_a = (-LRU_C) * r * _softplus(-lam_ref[d:d + 1, :])
        a = jnp.exp(log_a)
        u = jnp.sqrt(1.0 - jnp.exp(2.0 * log_a)) * (i * xc)
        return a, u

    width = xc_ref.shape[1]

    def fwd(c, carry):
        t0 = pl.multiple_of(c * ch, 8)
        a, u = gates(t0, 0)
        r = lax.broadcasted_iota(jnp.int32, u.shape, 0) + t0
        u = jnp.where(r >= N_PAD, u, 0.0)
        for g in range(ch // 8):
            hs = _group_scan(a[8 * g:8 * g + 8], u[8 * g:8 * g + 8], carry, False)
            y_ref[pl.ds(t0 + 8 * g, 8), :] = hs
            carry = hs[7:8]
        return carry

    lax.fori_loop(0, n_chunks, fwd, jnp.zeros((1, width), F32))

    def bwd(ci, carry):
        t0 = pl.multiple_of((n_chunks - 1 - ci) * ch, 8)
        a, u = gates(t0, 1)
        for g in reversed(range(ch // 8)):
            hs = _group_scan(a[8 * g:8 * g + 8], u[8 * g:8 * g + 8], carry, True)
            rows = pl.ds(t0 + 8 * g, 8)
            o_ref[rows, :] = ((y_ref[rows, :] + hs) * gate_ref[rows, :].astype(F32)).astype(BF16)
            carry = hs[0:1]
        return carry

    lax.fori_loop(0, n_chunks, bwd, jnp.zeros((1, width), F32))


def _lru_scan(xb, gate, cw, cb, wg, bg, lam):
    b, lp, _ = xb.shape
    w = LRU_BLOCK_W
    blk = pl.BlockSpec((None, lp, w), lambda i, j: (i, 0, j))
    col = lambda r: pl.BlockSpec((r, w), lambda i, j: (0, j))
    return pl.pallas_call(
        functools.partial(_lru_scan_kernel, lp=lp),
        grid=(b, LRU_BLOCKS),
        in_specs=[blk, blk, col(4), col(1),
                  pl.BlockSpec((2, 2, None, w, w), lambda i, j: (0, 0, j, 0, 0)),
                  col(4), col(2)],
        out_specs=blk,
        out_shape=jax.ShapeDtypeStruct((b, lp, LRU_WIDTH), BF16),
        scratch_shapes=[pltpu.VMEM((lp, w), F32), pltpu.VMEM((lp, w), F32)],
        compiler_params=_cparams(2),
        name="lru_scan",
    )(xb, gate, cw, cb, wg, bg, lam)


def _post_kernel(o_ref, h_ref, wo_ref, g_ref, win_ref, wout_ref, out_ref):
    g = g_ref[...]
    x = h_ref[...] + _rms(_dot(o_ref[...], wo_ref[...]), g[1:2])
    xn = _rms(x, g[2:3]).astype(BF16)
    acc = None
    for c in range(FFN_HIDDEN // FFN_CHUNK):
        lo = c * FFN_CHUNK
        gg = _dot(xn, win_ref[:, lo:lo + FFN_CHUNK])
        uu = _dot(xn, win_ref[:, FFN_HIDDEN + lo:FFN_HIDDEN + lo + FFN_CHUNK])
        a = (gg * jax.nn.sigmoid(gg) * uu).astype(BF16)
        part = _dot(a, wout_ref[lo:lo + FFN_CHUNK, :])
        acc = part if acc is None else acc + part
    out_ref[...] = x + _rms(acc, g[3:4])


def _post(o, h, w_o, g, w_in, w_out):
    rows = h.shape[0]
    tm = ROW_TILE
    row = lambda i: (i, 0)
    const = lambda a: pl.BlockSpec(a.shape, lambda i: (0, 0), pipeline_mode=pl.Buffered(1))
    return pl.pallas_call(
        _post_kernel,
        grid=(rows // tm,),
        in_specs=[pl.BlockSpec((tm, o.shape[1]), row), pl.BlockSpec((tm, D_MODEL), row),
                  const(w_o), _full(g.shape), const(w_in), const(w_out)],
        out_specs=pl.BlockSpec((tm, D_MODEL), row),
        out_shape=jax.ShapeDtypeStruct((rows, D_MODEL), F32),
        compiler_params=_cparams(1),
        name="post_ffn",
    )(o, h, w_o, g, w_in, w_out)


def _rope_rows(lp):
    pos = jnp.arange(lp, dtype=F32) - float(N_PAD)
    inv_freq = ROPE_THETA ** (-jnp.arange(0, MLA_ROPE, 2, dtype=F32) / MLA_ROPE)
    ang = pos[:, None] * inv_freq[None, :]
    c, s = jnp.cos(ang), jnp.sin(ang)
    z32 = jnp.zeros_like(c)
    z64 = jnp.zeros((lp, 64), F32)
    return (jnp.concatenate([c, c, z64], axis=1),
            jnp.concatenate([-s, z32, z64], axis=1),
            jnp.concatenate([z32, s, z64], axis=1))


def kernel(x, meta_tokens, norm_g, mla_w_in, mla_q_norm, mla_kv_norm, mla_w_uq, mla_w_ukv, mla_w_o,
           diff_w_in, diff_lambda, diff_subln, diff_w_o, lru_w_in, lru_conv_w, lru_conv_b,
           lru_w_gates, lru_b_gates, lru_lambda, lru_w_o, ffn_w_in, ffn_w_out):
    b, seq, d = x.shape
    depth = norm_g.shape[0]
    lp = seq + ROW_OFF
    assert d == D_MODEL and seq % 128 == 0 and lp % LRU_CHUNK == 0 and lp % ROW_TILE == 0

    meta = jnp.broadcast_to(meta_tokens[None].astype(x.dtype), (b, N_META, d))
    h = jnp.concatenate([jnp.zeros((b, N_PAD, d), x.dtype), meta, x], axis=1).reshape(b * lp, d)
    cos, slo, shi = _rope_rows(lp)
    slopes = 2.0 ** (-8.0 * jnp.arange(1, DIFF_HEADS + 1, dtype=F32) / DIFF_HEADS)

    for i in range(depth):
        kind, j = i % 3, i // 3
        g = norm_g[i]
        if kind == 0:
            w_in = jnp.pad(mla_w_in[j], ((0, 0), (0, 64))).astype(BF16)
            uq = mla_w_uq[j].reshape(MLA_Q_RANK, MLA_HEADS, MLA_NOPE + MLA_ROPE)
            uq = jnp.pad(uq, ((0, 0), (0, 0), (0, MLA_QK - MLA_NOPE - MLA_ROPE)))
            uq = uq.reshape(MLA_Q_RANK, MLA_HEADS * MLA_QK).astype(BF16)
            ukv = mla_w_ukv[j].reshape(MLA_KV_RANK, MLA_HEADS, MLA_NOPE + MLA_V)
            uk = ukv[:, :, :MLA_NOPE].reshape(MLA_KV_RANK, MLA_HEADS * MLA_NOPE).astype(BF16)
            uv = ukv[:, :, MLA_NOPE:].reshape(MLA_KV_RANK, MLA_HEADS * MLA_V).astype(BF16)
            q, k, v = _mla_prep(h, g[0:1], w_in, mla_q_norm[j][None], mla_kv_norm[j][None],
                                uq, uk, uv, cos, slo, shi, lp)
            o = _mla_attn(q.reshape(b, lp, -1), k.reshape(b, lp, -1), v.reshape(b, lp, -1), seq)
            w_o = mla_w_o[j]
        elif kind == 1:
            lambda_init = 0.8 - 0.6 * math.exp(-0.3 * i)
            q, k, v = _diff_prep(h, g[0:1], diff_w_in[j].astype(BF16))
            o = _diff_attn(slopes, q.reshape(b, lp, -1), k.reshape(b, lp, -1), v.reshape(b, lp, -1),
                           diff_lambda[j], diff_subln[j][None], seq, lambda_init)
            w_o = diff_w_o[j]
        else:
            xb, gate = _lru_prep(h, g[0:1], lru_w_in[j].astype(BF16))
            o = _lru_scan(xb.reshape(b, lp, -1), gate.reshape(b, lp, -1), lru_conv_w[j],
                          lru_conv_b[j][None], lru_w_gates[j].astype(BF16),
                          lru_b_gates[j].reshape(4, LRU_WIDTH), lru_lambda[j])
            w_o = lru_w_o[j]
        h = _post(o.reshape(b * lp, -1), h, w_o.astype(BF16), g,
                  ffn_w_in[i].astype(BF16), ffn_w_out[i].astype(BF16))
    return h.reshape(b, lp, d)[:, ROW_OFF:]
```

```python
import functools
import math

import jax
import jax.numpy as jnp
import numpy as np
from jax import lax
from jax.experimental import pallas as pl
from jax.experimental.pallas import tpu as pltpu

F32 = jnp.float32
BF16 = jnp.bfloat16

D_MODEL = 1024
N_META = 16
ROW_OFF = 128
N_PAD = ROW_OFF - N_META
NORM_EPS = 1e-6
LOG2E = math.log2(math.e)
NEG_BIG = -1e30

MLA_HEADS = 16
MLA_Q_RANK = 384
MLA_KV_RANK = 256
MLA_NOPE = 128
MLA_ROPE = 64
MLA_V = 128
MLA_QK = 256
ROPE_THETA = 10000.0

DIFF_HEADS = 8
DIFF_HEAD_DIM = 64

LRU_WIDTH = 1536
LRU_BLOCKS = 6
LRU_BLOCK_W = 256
LRU_C = 8.0
LRU_CHUNK = 384

FFN_HIDDEN = 2816
FFN_CHUNK = 1408

ROW_TILE = 384
VMEM_LIMIT = 56 * 1024 * 1024


def _cparams(n_axes):
    return pltpu.CompilerParams(dimension_semantics=("arbitrary",) * n_axes,
                                vmem_limit_bytes=VMEM_LIMIT)


def _rms(x, g):
    return x * lax.rsqrt(jnp.mean(x * x, axis=-1, keepdims=True) + NORM_EPS) * g


def _dot(a, b):
    return jnp.dot(a, b, preferred_element_type=F32)


def _dot_nt(a, b):
    return lax.dot_general(a, b, (((1,), (1,)), ((), ())), preferred_element_type=F32)


def _full(shape):
    nd = len(shape)
    return pl.BlockSpec(shape, lambda *_: (0,) * nd)


def _rope128(v, cos, sin_lo, sin_hi):
    return v * cos + pltpu.roll(v, 96, 1) * sin_lo + pltpu.roll(v, 32, 1) * sin_hi


def _mla_prep_kernel(h_ref, g_ref, win_ref, qg_ref, kvg_ref, wuq_ref, wuk_ref, wuv_ref,
                     cos_ref, slo_ref, shi_ref, q_ref, k_ref, v_ref, *, q_scale):
    u = _rms(h_ref[...], g_ref[...]).astype(BF16)
    t = _dot(u, win_ref[...])
    cq = _rms(t[:, :MLA_Q_RANK], qg_ref[...]).astype(BF16)
    ckv = _rms(t[:, MLA_Q_RANK:MLA_Q_RANK + MLA_KV_RANK], kvg_ref[...]).astype(BF16)
    cos, slo, shi = cos_ref[...], slo_ref[...], shi_ref[...]
    kr = _rope128(t[:, MLA_Q_RANK + MLA_KV_RANK:], cos, slo, shi).astype(BF16)
    for hp in range(MLA_HEADS // 2):
        kn = _dot(ckv, wuk_ref[:, hp * 256:(hp + 1) * 256])
        v_ref[:, hp * 256:(hp + 1) * 256] = _dot(ckv, wuv_ref[:, hp * 256:(hp + 1) * 256]).astype(BF16)
        for j in range(2):
            h = 2 * hp + j
            q = _dot(cq, wuq_ref[:, h * MLA_QK:(h + 1) * MLA_QK])
            q_ref[:, h * MLA_QK:h * MLA_QK + 128] = (q[:, :128] * q_scale).astype(BF16)
            q_ref[:, h * MLA_QK + 128:(h + 1) * MLA_QK] = (
                _rope128(q[:, 128:], cos, slo, shi) * q_scale).astype(BF16)
            k_ref[:, h * MLA_QK:h * MLA_QK + 128] = kn[:, j * 128:(j + 1) * 128].astype(BF16)
            k_ref[:, h * MLA_QK + 128:(h + 1) * MLA_QK] = kr


def _mla_prep(h, g, w_in, qg, kvg, wuq, wuk, wuv, cos, slo, shi, lp):
    rows = h.shape[0]
    tm = ROW_TILE
    tiles_per_seq = lp // tm
    row = lambda i: (i, 0)
    pos = lambda i: (i % tiles_per_seq, 0)
    q_scale = (MLA_NOPE + MLA_ROPE) ** -0.5 * LOG2E
    return pl.pallas_call(
        functools.partial(_mla_prep_kernel, q_scale=q_scale),
        grid=(rows // tm,),
        in_specs=[pl.BlockSpec((tm, D_MODEL), row), _full(g.shape), _full(w_in.shape),
                  _full(qg.shape), _full(kvg.shape), _full(wuq.shape), _full(wuk.shape),
                  _full(wuv.shape), pl.BlockSpec((tm, 128), pos), pl.BlockSpec((tm, 128), pos),
                  pl.BlockSpec((tm, 128), pos)],
        out_specs=[pl.BlockSpec((tm, MLA_HEADS * MLA_QK), row),
                   pl.BlockSpec((tm, MLA_HEADS * MLA_QK), row),
                   pl.BlockSpec((tm, MLA_HEADS * MLA_V), row)],
        out_shape=[jax.ShapeDtypeStruct((rows, MLA_HEADS * MLA_QK), BF16),
                   jax.ShapeDtypeStruct((rows, MLA_HEADS * MLA_QK), BF16),
                   jax.ShapeDtypeStruct((rows, MLA_HEADS * MLA_V), BF16)],
        compiler_params=_cparams(1),
        name="mla_prep",
    )(h, g, w_in, qg, kvg, wuq, wuk, wuv, cos, slo, shi)


def _v_ones(v_ref, k0, n):
    return jnp.concatenate([v_ref[k0:k0 + n, :], jnp.ones((n, v_ref.shape[1]), BF16)], axis=1)


def _query_tiles(q_tile, o_ref, seq, tq):
    o_ref[0:N_PAD, :] = jnp.zeros((N_PAD, o_ref.shape[1]), o_ref.dtype)
    q_tile(N_PAD, N_META + tq)

    def body(i, carry):
        q_tile(pl.multiple_of(ROW_OFF + i * tq, 128), tq)
        return carry

    lax.fori_loop(1, seq // tq, body, 0)


def _pick_tile(seq, largest):
    t = largest
    while seq % t:
        t //= 2
    return t


def _mla_attn_kernel(q_ref, k_ref, v_ref, o_ref, *, seq, tq, tk):
    dv = v_ref.shape[1]

    def q_tile(r0, rows):
        q = q_ref[pl.ds(r0, rows), :]
        s = _dot_nt(q, k_ref[0:ROW_OFF, :])
        col = lax.broadcasted_iota(jnp.int32, (rows, ROW_OFF), 1)
        s = jnp.where(col >= N_PAD, s, NEG_BIG)
        m = jnp.max(s, axis=1, keepdims=True)
        acc = _dot(jnp.exp2(s - m).astype(BF16), _v_ones(v_ref, 0, ROW_OFF))
        for c in range(seq // tk):
            k0 = ROW_OFF + c * tk
            s = _dot_nt(q, k_ref[k0:k0 + tk, :])
            m_new = jnp.maximum(m, jnp.max(s, axis=1, keepdims=True))
            p = jnp.exp2(s - m_new)
            acc = jnp.exp2(m - m_new) * acc + _dot(p.astype(BF16), _v_ones(v_ref, k0, tk))
            m = m_new
        o_ref[pl.ds(r0, rows), :] = (acc[:, :dv] / acc[:, dv:]).astype(BF16)

    _query_tiles(q_tile, o_ref, seq, tq)


def _mla_attn(q, k, v, seq):
    b, lp, _ = q.shape
    tq = _pick_tile(seq, 1024)
    tk = _pick_tile(seq, 256)
    return pl.pallas_call(
        functools.partial(_mla_attn_kernel, seq=seq, tq=tq, tk=tk),
        grid=(b, MLA_HEADS),
        in_specs=[pl.BlockSpec((None, lp, MLA_QK), lambda i, j: (i, 0, j)),
                  pl.BlockSpec((None, lp, MLA_QK), lambda i, j: (i, 0, j)),
                  pl.BlockSpec((None, lp, MLA_V), lambda i, j: (i, 0, j))],
        out_specs=pl.BlockSpec((None, lp, MLA_V), lambda i, j: (i, 0, j)),
        out_shape=jax.ShapeDtypeStruct((b, lp, MLA_HEADS * MLA_V), BF16),
        compiler_params=_cparams(2),
        name="mla_attn",
    )(q, k, v)


def _diff_prep_kernel(h_ref, g_ref, w_ref, q_ref, k_ref, v_ref, *, q_scale):
    u = _rms(h_ref[...], g_ref[...]).astype(BF16)
    n = q_ref.shape[1]
    for c in range(n // 256):
        sl = slice(c * 256, (c + 1) * 256)
        q_ref[:, sl] = (_dot(u, w_ref[:, c * 256:(c + 1) * 256]) * q_scale).astype(BF16)
        k_ref[:, sl] = _dot(u, w_ref[:, n + c * 256:n + (c + 1) * 256]).astype(BF16)
        v_ref[:, sl] = _dot(u, w_ref[:, 2 * n + c * 256:2 * n + (c + 1) * 256]).astype(BF16)


def _diff_prep(h, g, w):
    rows = h.shape[0]
    tm = ROW_TILE
    n = w.shape[1] // 3
    row = lambda i: (i, 0)
    return pl.pallas_call(
        functools.partial(_diff_prep_kernel, q_scale=DIFF_HEAD_DIM ** -0.5 * LOG2E),
        grid=(rows // tm,),
        in_specs=[pl.BlockSpec((tm, D_MODEL), row), _full(g.shape), _full(w.shape)],
        out_specs=[pl.BlockSpec((tm, n), row)] * 3,
        out_shape=[jax.ShapeDtypeStruct((rows, n), BF16)] * 3,
        compiler_params=_cparams(1),
        name="diff_prep",
    )(h, g, w)


def _diff_attn_kernel(slope_ref, q_ref, k_ref, v_ref, lam_ref, g_ref, o_ref, *,
                      seq, tq, tk, lambda_init):
    slope = slope_ref[pl.program_id(1)] * LOG2E
    lp4 = lam_ref[...]
    lam = (jnp.exp(jnp.sum(lp4[0:1] * lp4[1:2], axis=1, keepdims=True))
           - jnp.exp(jnp.sum(lp4[2:3] * lp4[3:4], axis=1, keepdims=True)) + lambda_init)
    g = g_ref[...]

    def q_tile(r0, rows):
        q = q_ref[pl.ds(r0, rows), :]
        lane = lax.broadcasted_iota(jnp.int32, q.shape, 1)
        zero = jnp.zeros_like(q)
        qs = (jnp.where(lane < DIFF_HEAD_DIM, q, zero), jnp.where(lane >= DIFF_HEAD_DIM, q, zero))

        def scores(k0, width, mask_pad):
            kk = k_ref[k0:k0 + width, :]
            d = (lax.broadcasted_iota(jnp.int32, (rows, width), 0)
                 - lax.broadcasted_iota(jnp.int32, (rows, width), 1)).astype(F32)
            off = r0 - k0
            off = float(off) if isinstance(off, int) else off.astype(F32)
            bias = jnp.abs(d + off) * slope
            out = []
            for c in range(2):
                s = _dot_nt(qs[c], kk) - bias
                if mask_pad:
                    col = lax.broadcasted_iota(jnp.int32, (rows, width), 1)
                    s = jnp.where(col >= N_PAD, s, NEG_BIG)
                out.append(s)
            return out

        s2 = scores(0, ROW_OFF, True)
        vv = _v_ones(v_ref, 0, ROW_OFF)
        m, acc = [], []
        for c in range(2):
            mc = jnp.max(s2[c], axis=1, keepdims=True)
            m.append(mc)
            acc.append(_dot(jnp.exp2(s2[c] - mc).astype(BF16), vv))
        for ch in range(seq // tk):
            k0 = ROW_OFF + ch * tk
            s2 = scores(k0, tk, False)
            vv = _v_ones(v_ref, k0, tk)
            for c in range(2):
                m_new = jnp.maximum(m[c], jnp.max(s2[c], axis=1, keepdims=True))
                p = jnp.exp2(s2[c] - m_new)
                acc[c] = jnp.exp2(m[c] - m_new) * acc[c] + _dot(p.astype(BF16), vv)
                m[c] = m_new
        dv = v_ref.shape[1]
        o = acc[0][:, :dv] / acc[0][:, dv:] - lam * (acc[1][:, :dv] / acc[1][:, dv:])
        o = _rms(o, g) * (1.0 - lambda_init)
        o_ref[pl.ds(r0, rows), :] = o.astype(BF16)

    _query_tiles(q_tile, o_ref, seq, tq)


def _diff_attn(slopes, q, k, v, lam_p, g, seq, lambda_init):
    b, lp, _ = q.shape
    tq = _pick_tile(seq, 512)
    tk = _pick_tile(seq, 256)
    w = 2 * DIFF_HEAD_DIM
    blk = pl.BlockSpec((None, lp, w), lambda i, j: (i, 0, j))
    return pl.pallas_call(
        functools.partial(_diff_attn_kernel, seq=seq, tq=tq, tk=tk, lambda_init=lambda_init),
        grid=(b, DIFF_HEADS),
        in_specs=[pl.BlockSpec(memory_space=pltpu.SMEM), blk, blk, blk,
                  _full(lam_p.shape), _full(g.shape)],
        out_specs=blk,
        out_shape=jax.ShapeDtypeStruct((b, lp, DIFF_HEADS * w), BF16),
        compiler_params=_cparams(2),
        name="diff_attn",
    )(slopes, q, k, v, lam_p, g)


def _gelu_tanh(x):
    c = math.sqrt(2.0 / math.pi)
    return x * (0.5 * (1.0 + jnp.tanh(c * (x + 0.044715 * (x * x * x)))))


def _lru_prep_kernel(h_ref, g_ref, w_ref, xb_ref, gate_ref):
    u = _rms(h_ref[...], g_ref[...]).astype(BF16)
    for c in range(LRU_WIDTH // 256):
        sl = slice(c * 256, (c + 1) * 256)
        xb_ref[:, sl] = _dot(u, w_ref[:, c * 256:(c + 1) * 256])
        gate = _dot(u, w_ref[:, LRU_WIDTH + c * 256:LRU_WIDTH + (c + 1) * 256])
        gate_ref[:, sl] = _gelu_tanh(gate).astype(BF16)


def _lru_prep(h, g, w):
    rows = h.shape[0]
    tm = ROW_TILE
    row = lambda i: (i, 0)
    return pl.pallas_call(
        _lru_prep_kernel,
        grid=(rows // tm,),
        in_specs=[pl.BlockSpec((tm, D_MODEL), row), _full(g.shape), _full(w.shape)],
        out_specs=[pl.BlockSpec((tm, LRU_WIDTH), row)] * 2,
        out_shape=[jax.ShapeDtypeStruct((rows, LRU_WIDTH), F32),
                   jax.ShapeDtypeStruct((rows, LRU_WIDTH), BF16)],
        compiler_params=_cparams(1),
        name="lru_prep",
    )(h, g, w)


def _softplus(x):
    return jnp.maximum(x, 0.0) + jnp.log(1.0 + jnp.exp(-jnp.abs(x)))


def _group_scan(a, u, carry, reverse):
    row = lax.broadcasted_iota(jnp.int32, a.shape, 0)
    for k in (1, 2, 4):
        shift = (8 - k) if reverse else k
        keep = (row < 8 - k) if reverse else (row >= k)
        a_s = pltpu.roll(a, shift, 0)
        u_s = pltpu.roll(u, shift, 0)
        u = jnp.where(keep, a * u_s + u, u)
        a = jnp.where(keep, a * a_s, a)
    return a * carry + u


def _lru_scan_kernel(xb_ref, gate_ref, cw_ref, cb_ref, wg_ref, bg_ref, lam_ref, o_ref,
                     xc_ref, y_ref, *, lp):
    ch = LRU_CHUNK
    n_chunks = lp // ch
    cw = cw_ref[...]
    cb = cb_ref[...]

    for c in range(n_chunks):
        t0 = c * ch
        lo = 8 if c == 0 else 0
        hi = ch - 8 if c == n_chunks - 1 else ch
        n = hi - lo
        acc = cb
        for j in range(4):
            x = xb_ref[t0 + lo + j - 2:t0 + lo + j - 2 + n, :]
            if c == 0:
                r = lax.broadcasted_iota(jnp.int32, x.shape, 0) + (lo + j - 2)
                x = jnp.where(r >= N_PAD, x, 0.0)
            acc = acc + cw[j:j + 1] * x
        xc_ref[t0 + lo:t0 + hi, :] = acc
        if c == 0:
            xc_ref[0:8, :] = jnp.zeros((8, xc_ref.shape[1]), F32)
        if c == n_chunks - 1:
            base = lp - 8
            tail = xb_ref[base:lp, :]
            prev = xb_ref[base - 8:base, :]
            r = lax.broadcasted_iota(jnp.int32, tail.shape, 0)
            x_m2 = jnp.where(r >= 2, pltpu.roll(tail, 2, 0), pltpu.roll(prev, 2, 0))
            x_m1 = jnp.where(r >= 1, pltpu.roll(tail, 1, 0), pltpu.roll(prev, 1, 0))
            x_p1 = jnp.where(r < 7, pltpu.roll(tail, 7, 0), 0.0)
            xc_ref[base:lp, :] = (cb + cw[0:1] * x_m2 + cw[1:2] * x_m1
                                  + cw[2:3] * tail + cw[3:4] * x_p1)

    def gates(t0, d):
        xc = xc_ref[pl.ds(t0, ch), :]
        xcb = xc.astype(BF16)
        r = jax.nn.sigmoid(_dot(xcb, wg_ref[d, 0]) + bg_ref[2 * d:2 * d + 1, :])
        i = jax.nn.sigmoid(_dot(xcb, wg_ref[d, 1]) + bg_ref[2 * d + 1:2 * d + 2, :])
        log_a = (-LRU_C) * r * _softplus(-lam_ref[d:d + 1, :])
        a = jnp.exp(log_a)
        u = jnp.sqrt(1.0 - jnp.exp(2.0 * log_a)) * (i * xc)
        return a, u

    width = xc_ref.shape[1]

    def fwd(c, carry):
        t0 = pl.multiple_of(c * ch, 8)
        a, u = gates(t0, 0)
        r = lax.broadcasted_iota(jnp.int32, u.shape, 0) + t0
        u = jnp.where(r >= N_PAD, u, 0.0)
        for g in range(ch // 8):
            hs = _group_scan(a[8 * g:8 * g + 8], u[8 * g:8 * g + 8], carry, False)
            y_ref[pl.ds(t0 + 8 * g, 8), :] = hs
            carry = hs[7:8]
        return carry

    lax.fori_loop(0, n_chunks, fwd, jnp.zeros((1, width), F32))

    def bwd(ci, carry):
        t0 = pl.multiple_of((n_chunks - 1 - ci) * ch, 8)
        a, u = gates(t0, 1)
        for g in reversed(range(ch // 8)):
            hs = _group_scan(a[8 * g:8 * g + 8], u[8 * g:8 * g + 8], carry, True)
            rows = pl.ds(t0 + 8 * g, 8)
            o_ref[rows, :] = ((y_ref[rows, :] + hs) * gate_ref[rows, :].astype(F32)).astype(BF16)
            carry = hs[0:1]
        return carry

    lax.fori_loop(0, n_chunks, bwd, jnp.zeros((1, width), F32))


def _lru_scan(xb, gate, cw, cb, wg, bg, lam):
    b, lp, _ = xb.shape
    w = LRU_BLOCK_W
    blk = pl.BlockSpec((None, lp, w), lambda i, j: (i, 0, j))
    col = lambda r: pl.BlockSpec((r, w), lambda i, j: (0, j))
    return pl.pallas_call(
        functools.partial(_lru_scan_kernel, lp=lp),
        grid=(b, LRU_BLOCKS),
        in_specs=[blk, blk, col(4), col(1),
                  pl.BlockSpec((2, 2, None, w, w), lambda i, j: (0, 0, j, 0, 0)),
                  col(4), col(2)],
        out_specs=blk,
        out_shape=jax.ShapeDtypeStruct((b, lp, LRU_WIDTH), BF16),
        scratch_shapes=[pltpu.VMEM((lp, w), F32), pltpu.VMEM((lp, w), F32)],
        compiler_params=_cparams(2),
        name="lru_scan",
    )(xb, gate, cw, cb, wg, bg, lam)


def _post_kernel(o_ref, h_ref, wo_ref, g_ref, win_ref, wout_ref, out_ref):
    g = g_ref[...]
    x = h_ref[...] + _rms(_dot(o_ref[...], wo_ref[...]), g[1:2])
    xn = _rms(x, g[2:3]).astype(BF16)
    acc = None
    for c in range(FFN_HIDDEN // FFN_CHUNK):
        lo = c * FFN_CHUNK
        gg = _dot(xn, win_ref[:, lo:lo + FFN_CHUNK])
        uu = _dot(xn, win_ref[:, FFN_HIDDEN + lo:FFN_HIDDEN + lo + FFN_CHUNK])
        a = (gg * jax.nn.sigmoid(gg) * uu).astype(BF16)
        part = _dot(a, wout_ref[lo:lo + FFN_CHUNK, :])
        acc = part if acc is None else acc + part
    out_ref[...] = x + _rms(acc, g[3:4])


def _post(o, h, w_o, g, w_in, w_out):
    rows = h.shape[0]
    tm = ROW_TILE
    row = lambda i: (i, 0)
    const = lambda a: pl.BlockSpec(a.shape, lambda i: (0, 0), pipeline_mode=pl.Buffered(1))
    return pl.pallas_call(
        _post_kernel,
        grid=(rows // tm,),
        in_specs=[pl.BlockSpec((tm, o.shape[1]), row), pl.BlockSpec((tm, D_MODEL), row),
                  const(w_o), _full(g.shape), const(w_in), const(w_out)],
        out_specs=pl.BlockSpec((tm, D_MODEL), row),
        out_shape=jax.ShapeDtypeStruct((rows, D_MODEL), F32),
        compiler_params=_cparams(1),
        name="post_ffn",
    )(o, h, w_o, g, w_in, w_out)


def _rope_rows(lp):
    pos = jnp.arange(lp, dtype=F32) - float(N_PAD)
    inv_freq = ROPE_THETA ** (-jnp.arange(0, MLA_ROPE, 2, dtype=F32) / MLA_ROPE)
    ang = pos[:, None] * inv_freq[None, :]
    c, s = jnp.cos(ang), jnp.sin(ang)
    z32 = jnp.zeros_like(c)
    z64 = jnp.zeros((lp, 64), F32)
    return (jnp.concatenate([c, c, z64], axis=1),
            jnp.concatenate([-s, z32, z64], axis=1),
            jnp.concatenate([z32, s, z64], axis=1))


def kernel(x, meta_tokens, norm_g, mla_w_in, mla_q_norm, mla_kv_norm, mla_w_uq, mla_w_ukv, mla_w_o,
           diff_w_in, diff_lambda, diff_subln, diff_w_o, lru_w_in, lru_conv_w, lru_conv_b,
           lru_w_gates, lru_b_gates, lru_lambda, lru_w_o, ffn_w_in, ffn_w_out):
    b, seq, d = x.shape
    depth = norm_g.shape[0]
    lp = seq + ROW_OFF
    assert d == D_MODEL and seq % 128 == 0 and lp % LRU_CHUNK == 0 and lp % ROW_TILE == 0

    meta = jnp.broadcast_to(meta_tokens[None].astype(x.dtype), (b, N_META, d))
    h = jnp.concatenate([jnp.zeros((b, N_PAD, d), x.dtype), meta, x], axis=1).reshape(b * lp, d)
    cos, slo, shi = _rope_rows(lp)
    slopes = 2.0 ** (-8.0 * jnp.arange(1, DIFF_HEADS + 1, dtype=F32) / DIFF_HEADS)

    for i in range(depth):
        kind, j = i % 3, i // 3
        g = norm_g[i]
        if kind == 0:
            w_in = jnp.pad(mla_w_in[j], ((0, 0), (0, 64))).astype(BF16)
            uq = mla_w_uq[j].reshape(MLA_Q_RANK, MLA_HEADS, MLA_NOPE + MLA_ROPE)
            uq = jnp.pad(uq, ((0, 0), (0, 0), (0, MLA_QK - MLA_NOPE - MLA_ROPE)))
            uq = uq.reshape(MLA_Q_RANK, MLA_HEADS * MLA_QK).astype(BF16)
            ukv = mla_w_ukv[j].reshape(MLA_KV_RANK, MLA_HEADS, MLA_NOPE + MLA_V)
            uk = ukv[:, :, :MLA_NOPE].reshape(MLA_KV_RANK, MLA_HEADS * MLA_NOPE).astype(BF16)
            uv = ukv[:, :, MLA_NOPE:].reshape(MLA_KV_RANK, MLA_HEADS * MLA_V).astype(BF16)
            q, k, v = _mla_prep(h, g[0:1], w_in, mla_q_norm[j][None], mla_kv_norm[j][None],
                                uq, uk, uv, cos, slo, shi, lp)
            o = _mla_attn(q.reshape(b, lp, -1), k.reshape(b, lp, -1), v.reshape(b, lp, -1), seq)
            w_o = mla_w_o[j]
        elif kind == 1:
            lambda_init = 0.8 - 0.6 * math.exp(-0.3 * i)
            q, k, v = _diff_prep(h, g[0:1], diff_w_in[j].astype(BF16))
            o = _diff_attn(slopes, q.reshape(b, lp, -1), k.reshape(b, lp, -1), v.reshape(b, lp, -1),
                           diff_lambda[j], diff_subln[j][None], seq, lambda_init)
            w_o = diff_w_o[j]
        else:
            xb, gate = _lru_prep(h, g[0:1], lru_w_in[j].astype(BF16))
            o = _lru_scan(xb.reshape(b, lp, -1), gate.reshape(b, lp, -1), lru_conv_w[j],
                          lru_conv_b[j][None], lru_w_gates[j].astype(BF16),
                          lru_b_gates[j].reshape(4, LRU_WIDTH), lru_lambda[j])
            w_o = lru_w_o[j]
        h = _post(o.reshape(b * lp, -1), h, w_o.astype(BF16), g,
                  ffn_w_in[i].astype(BF16), ffn_w_out[i].astype(BF16))
    return h.reshape(b, lp, d)[:, ROW_OFF:]
```

```python
import functools
import math

import jax
import jax.numpy as jnp
import numpy as np
from jax import lax
from jax.experimental import pallas as pl
from jax.experimental.pallas import tpu as pltpu

F32 = jnp.float32
BF16 = jnp.bfloat16

D_MODEL = 1024
N_META = 16
ROW_OFF = 128
N_PAD = ROW_OFF - N_META
NORM_EPS = 1e-6
LOG2E = math.log2(math.e)
NEG_BIG = -1e30

MLA_HEADS = 16
MLA_Q_RANK = 384
MLA_KV_RANK = 256
MLA_NOPE = 128
MLA_ROPE = 64
MLA_V = 128
MLA_QK = 256
ROPE_THETA = 10000.0

DIFF_HEADS = 8
DIFF_HEAD_DIM = 64

LRU_WIDTH = 1536
LRU_BLOCKS = 6
LRU_BLOCK_W = 256
LRU_C = 8.0
LRU_CHUNK = 384

FFN_HIDDEN = 2816
FFN_CHUNK = 1408

ROW_TILE = 384
VMEM_LIMIT = 56 * 1024 * 1024


def _cparams(n_axes):
    return pltpu.CompilerParams(dimension_semantics=("arbitrary",) * n_axes,
                                vmem_limit_bytes=VMEM_LIMIT)


def _rms(x, g):
    return x * lax.rsqrt(jnp.mean(x * x, axis=-1, keepdims=True) + NORM_EPS) * g


def _dot(a, b):
    return jnp.dot(a, b, preferred_element_type=F32)


def _dot_nt(a, b):
    return lax.dot_general(a, b, (((1,), (1,)), ((), ())), preferred_element_type=F32)


def _full(shape):
    nd = len(shape)
    return pl.BlockSpec(shape, lambda *_: (0,) * nd)


def _rope128(v, cos, sin_lo, sin_hi):
    return v * cos + pltpu.roll(v, 96, 1) * sin_lo + pltpu.roll(v, 32, 1) * sin_hi


def _mla_prep_kernel(h_ref, g_ref, win_ref, qg_ref, kvg_ref, wuq_ref, wuk_ref, wuv_ref,
                     cos_ref, slo_ref, shi_ref, q_ref, k_ref, v_ref, *, q_scale):
    u = _rms(h_ref[...], g_ref[...]).astype(BF16)
    t = _dot(u, win_ref[...])
    cq = _rms(t[:, :MLA_Q_RANK], qg_ref[...]).astype(BF16)
    ckv = _rms(t[:, MLA_Q_RANK:MLA_Q_RANK + MLA_KV_RANK], kvg_ref[...]).astype(BF16)
    cos, slo, shi = cos_ref[...], slo_ref[...], shi_ref[...]
    kr = _rope128(t[:, MLA_Q_RANK + MLA_KV_RANK:], cos, slo, shi).astype(BF16)
    for hp in range(MLA_HEADS // 2):
        kn = _dot(ckv, wuk_ref[:, hp * 256:(hp + 1) * 256])
        v_ref[:, hp * 256:(hp + 1) * 256] = _dot(ckv, wuv_ref[:, hp * 256:(hp + 1) * 256]).astype(BF16)
        for j in range(2):
            h = 2 * hp + j
            q = _dot(cq, wuq_ref[:, h * MLA_QK:(h + 1) * MLA_QK])
            q_ref[:, h * MLA_QK:h * MLA_QK + 128] = (q[:, :128] * q_scale).astype(BF16)
            q_ref[:, h * MLA_QK + 128:(h + 1) * MLA_QK] = (
                _rope128(q[:, 128:], cos, slo, shi) * q_scale).astype(BF16)
            k_ref[:, h * MLA_QK:h * MLA_QK + 128] = kn[:, j * 128:(j + 1) * 128].astype(BF16)
            k_ref[:, h * MLA_QK + 128:(h + 1) * MLA_QK] = kr


def _mla_prep(h, g, w_in, qg, kvg, wuq, wuk, wuv, cos, slo, shi, lp):
    rows = h.shape[0]
    tm = ROW_TILE
    tiles_per_seq = lp // tm
    row = lambda i: (i, 0)
    pos = lambda i: (i % tiles_per_seq, 0)
    q_scale = (MLA_NOPE + MLA_ROPE) ** -0.5 * LOG2E
    return pl.pallas_call(
        functools.partial(_mla_prep_kernel, q_scale=q_scale),
        grid=(rows // tm,),
        in_specs=[pl.BlockSpec((tm, D_MODEL), row), _full(g.shape), _full(w_in.shape),
                  _full(qg.shape), _full(kvg.shape), _full(wuq.shape), _full(wuk.shape),
                  _full(wuv.shape), pl.BlockSpec((tm, 128), pos), pl.BlockSpec((tm, 128), pos),
                  pl.BlockSpec((tm, 128), pos)],
        out_specs=[pl.BlockSpec((tm, MLA_HEADS * MLA_QK), row),
                   pl.BlockSpec((tm, MLA_HEADS * MLA_QK), row),
                   pl.BlockSpec((tm, MLA_HEADS * MLA_V), row)],
        out_shape=[jax.ShapeDtypeStruct((rows, MLA_HEADS * MLA_QK), BF16),
                   jax.ShapeDtypeStruct((rows, MLA_HEADS * MLA_QK), BF16),
                   jax.ShapeDtypeStruct((rows, MLA_HEADS * MLA_V), BF16)],
        compiler_params=_cparams(1),
        name="mla_prep",
    )(h, g, w_in, qg, kvg, wuq, wuk, wuv, cos, slo, shi)


def _v_ones(v_ref, k0, n):
    return jnp.concatenate([v_ref[pl.ds(k0, n), :], jnp.ones((n, v_ref.shape[1]), BF16)], axis=1)


def _query_tiles(q_tile, o_ref, seq, tq):
    o_ref[0:N_PAD, :] = jnp.zeros((N_PAD, o_ref.shape[1]), o_ref.dtype)
    q_tile(N_PAD, N_META + tq)

    def body(i, carry):
        q_tile(pl.multiple_of(ROW_OFF + i * tq, 128), tq)
        return carry

    lax.fori_loop(1, seq // tq, body, 0)


def _pick_tile(seq, largest):
    t = largest
    while seq % t:
        t //= 2
    return t


def _mla_attn_kernel(q_ref, k_ref, v_ref, o_ref, *, seq, tq, tk):
    dv = v_ref.shape[1]

    def q_tile(r0, rows):
        q = q_ref[pl.ds(r0, rows), :]
        s = _dot_nt(q, k_ref[0:ROW_OFF, :])
        col = lax.broadcasted_iota(jnp.int32, (rows, ROW_OFF), 1)
        s = jnp.where(col >= N_PAD, s, NEG_BIG)
        m = jnp.max(s, axis=1, keepdims=True)
        acc = _dot(jnp.exp2(s - m).astype(BF16), _v_ones(v_ref, 0, ROW_OFF))
        for c in range(seq // tk):
            k0 = ROW_OFF + c * tk
            s = _dot_nt(q, k_ref[k0:k0 + tk, :])
            m_new = jnp.maximum(m, jnp.max(s, axis=1, keepdims=True))
            p = jnp.exp2(s - m_new)
            acc = jnp.exp2(m - m_new) * acc + _dot(p.astype(BF16), _v_ones(v_ref, k0, tk))
            m = m_new
        o_ref[pl.ds(r0, rows), :] = (acc[:, :dv] / acc[:, dv:]).astype(BF16)

    _query_tiles(q_tile, o_ref, seq, tq)


def _mla_attn(q, k, v, seq):
    b, lp, _ = q.shape
    tq = _pick_tile(seq, 2048)
    tk = _pick_tile(seq, 256)
    return pl.pallas_call(
        functools.partial(_mla_attn_kernel, seq=seq, tq=tq, tk=tk),
        grid=(b, MLA_HEADS),
        in_specs=[pl.BlockSpec((None, lp, MLA_QK), lambda i, j: (i, 0, j)),
                  pl.BlockSpec((None, lp, MLA_QK), lambda i, j: (i, 0, j)),
                  pl.BlockSpec((None, lp, MLA_V), lambda i, j: (i, 0, j))],
        out_specs=pl.BlockSpec((None, lp, MLA_V), lambda i, j: (i, 0, j)),
        out_shape=jax.ShapeDtypeStruct((b, lp, MLA_HEADS * MLA_V), BF16),
        compiler_params=_cparams(2),
        name="mla_attn",
    )(q, k, v)


def _diff_prep_kernel(h_ref, g_ref, w_ref, q_ref, k_ref, v_ref, *, q_scale):
    u = _rms(h_ref[...], g_ref[...]).astype(BF16)
    n = q_ref.shape[1]
    for c in range(n // 256):
        sl = slice(c * 256, (c + 1) * 256)
        q_ref[:, sl] = (_dot(u, w_ref[:, c * 256:(c + 1) * 256]) * q_scale).astype(BF16)
        k_ref[:, sl] = _dot(u, w_ref[:, n + c * 256:n + (c + 1) * 256]).astype(BF16)
        v_ref[:, sl] = _dot(u, w_ref[:, 2 * n + c * 256:2 * n + (c + 1) * 256]).astype(BF16)


def _diff_prep(h, g, w):
    rows = h.shape[0]
    tm = ROW_TILE
    n = w.shape[1] // 3
    row = lambda i: (i, 0)
    return pl.pallas_call(
        functools.partial(_diff_prep_kernel, q_scale=DIFF_HEAD_DIM ** -0.5 * LOG2E),
        grid=(rows // tm,),
        in_specs=[pl.BlockSpec((tm, D_MODEL), row), _full(g.shape), _full(w.shape)],
        out_specs=[pl.BlockSpec((tm, n), row)] * 3,
        out_shape=[jax.ShapeDtypeStruct((rows, n), BF16)] * 3,
        compiler_params=_cparams(1),
        name="diff_prep",
    )(h, g, w)


BIAS_SPLIT = 32


def _split3(x):
    c1 = x.astype(BF16).astype(F32)
    c2 = (x - c1).astype(BF16).astype(F32)
    c3 = (x - c1 - c2).astype(BF16).astype(F32)
    return c1, c2, c3


def _diff_attn_kernel(slope_ref, q_ref, k_ref, v_ref, lam_ref, g_ref, o_ref, *,
                      seq, tq, tk, lambda_init):
    slope2 = slope_ref[pl.program_id(1)] * LOG2E
    lp4 = lam_ref[...]
    lam = (jnp.exp(jnp.sum(lp4[0:1] * lp4[1:2], axis=1, keepdims=True))
           - jnp.exp(jnp.sum(lp4[2:3] * lp4[3:4], axis=1, keepdims=True)) + lambda_init)
    g = g_ref[...]
    dv = v_ref.shape[1]
    n_chunks = seq // tk
    per_tile = tq // tk

    def lanes3(lane, x1, x2, x3):
        return jnp.where(lane % 3 == 0, x1, jnp.where(lane % 3 == 1, x2, x3))

    c1, c2, c3 = _split3(jnp.full((1, 128), slope2, F32))
    lane_k = lax.broadcasted_iota(jnp.int32, (tk, 128), 1)
    jj = lax.broadcasted_iota(jnp.int32, (tk, 128), 0).astype(F32)
    piece = lanes3(lane_k, c1, c2, c3)
    k_const = jnp.where(lane_k < 3, jj,
                        jnp.where(lane_k < 6, float(BIAS_SPLIT) * piece,
                                  jnp.where(lane_k < 9, piece, 0.0))).astype(BF16)

    def k_aug(k0, n, const):
        return jnp.concatenate([k_ref[pl.ds(k0, n), :], const], axis=1)

    def q_tile(r0, rows, t):
        first = isinstance(t, int)
        q = q_ref[pl.ds(r0, rows), :]
        lane = lax.broadcasted_iota(jnp.int32, q.shape, 1)
        zero = jnp.zeros_like(q)
        qs = (jnp.where(lane < DIFF_HEAD_DIM, q, zero), jnp.where(lane >= DIFF_HEAD_DIM, q, zero))
        ii = lax.broadcasted_iota(jnp.int32, (rows, 128), 0)
        if first:
            ii = ii - N_META
        lane_q = lax.broadcasted_iota(jnp.int32, (rows, 128), 1)
        in_a = (lane_q >= 3) & (lane_q < 6)
        base = jnp.where(lane_q < 3, lanes3(lane_q, c1, c2, c3),
                         jnp.where(in_a, -(ii >> 5).astype(F32),
                                   jnp.where(lane_q < 9, -(ii & (BIAS_SPLIT - 1)).astype(F32), 0.0))
                         ).astype(BF16)
        unit = jnp.where(in_a, -1.0, 0.0).astype(BF16)

        def q_aug(sgn, n_hi):
            tt = base + unit * (n_hi if isinstance(n_hi, float) else n_hi.astype(BF16))
            if not isinstance(sgn, float):
                tt = tt * sgn.astype(BF16)
            elif sgn != 1.0:
                tt = tt * sgn
            tt = tt.astype(BF16)
            return [jnp.concatenate([qs[c], tt], axis=1) for c in range(2)]

        def above_diag(width, off):
            d = (lax.broadcasted_iota(jnp.int32, (rows, width), 1)
                 - lax.broadcasted_iota(jnp.int32, (rows, width), 0))
            if first:
                d = d + N_META
            return jnp.maximum((d - off).astype(F32), 0.0) * (2.0 * slope2)

        base_row = ROW_OFF + t * tq
        n_meta = float(base_row // BIAS_SPLIT) if first else (base_row // BIAS_SPLIT).astype(F32)
        qa = q_aug(1.0, n_meta)
        kk = k_aug(0, ROW_OFF, k_const[0:ROW_OFF])
        col = lax.broadcasted_iota(jnp.int32, (rows, ROW_OFF), 1)
        vv = _v_ones(v_ref, 0, ROW_OFF)
        fix = above_diag(ROW_OFF, ROW_OFF) if first else None
        m, acc = [], []
        for c in range(2):
            s = _dot_nt(qa[c], kk)
            if first:
                s = s - fix
            s = jnp.where(col >= N_PAD, s, NEG_BIG)
            mc = jnp.max(s, axis=1, keepdims=True)
            m.append(mc)
            acc.append(_dot(jnp.exp2(s - mc).astype(BF16), vv))

        for rel in range(n_chunks):
            diag = rel < per_tile
            if first:
                sgn = 1.0 if diag else -1.0
                n_hi = float(-(rel * tk) // BIAS_SPLIT)
                k0 = ROW_OFF + rel * tk
            else:
                raw = t * per_tile + rel
                wrapped = raw >= n_chunks
                c_idx = jnp.where(wrapped, raw - n_chunks, raw)
                sgn = 1.0 if diag else jnp.where(wrapped, 1.0, -1.0)
                n_hi = ((t * per_tile - c_idx) * (tk // BIAS_SPLIT)).astype(F32)
                k0 = pl.multiple_of(ROW_OFF + c_idx * tk, 128)
            qa = q_aug(sgn, n_hi)
            kk = k_aug(k0, tk, k_const)
            vv = _v_ones(v_ref, k0, tk)
            fix = above_diag(tk, -rel * tk) if diag else None
            for c in range(2):
                s = _dot_nt(qa[c], kk)
                if diag:
                    s = s - fix
                m_new = jnp.maximum(m[c], jnp.max(s, axis=1, keepdims=True))
                p = jnp.exp2(s - m_new)
                acc[c] = jnp.exp2(m[c] - m_new) * acc[c] + _dot(p.astype(BF16), vv)
                m[c] = m_new
        o = acc[0][:, :dv] / acc[0][:, dv:] - lam * (acc[1][:, :dv] / acc[1][:, dv:])
        o = _rms(o, g) * (1.0 - lambda_init)
        o_ref[pl.ds(r0, rows), :] = o.astype(BF16)

    o_ref[0:N_PAD, :] = jnp.zeros((N_PAD, o_ref.shape[1]), o_ref.dtype)
    q_tile(N_PAD, N_META + tq, 0)

    def body(i, carry):
        q_tile(pl.multiple_of(ROW_OFF + i * tq, 128), tq, i)
        return carry

    lax.fori_loop(1, seq // tq, body, 0)


def _diff_attn(slopes, q, k, v, lam_p, g, seq, lambda_init):
    b, lp, _ = q.shape
    tq = _pick_tile(seq, 512)
    tk = _pick_tile(seq, 256)
    w = 2 * DIFF_HEAD_DIM
    blk = pl.BlockSpec((None, lp, w), lambda i, j: (i, 0, j))
    return pl.pallas_call(
        functools.partial(_diff_attn_kernel, seq=seq, tq=tq, tk=tk, lambda_init=lambda_init),
        grid=(b, DIFF_HEADS),
        in_specs=[pl.BlockSpec(memory_space=pltpu.SMEM), blk, blk, blk,
                  _full(lam_p.shape), _full(g.shape)],
        out_specs=blk,
        out_shape=jax.ShapeDtypeStruct((b, lp, DIFF_HEADS * w), BF16),
        compiler_params=_cparams(2),
        name="diff_attn",
    )(slopes, q, k, v, lam_p, g)


def _gelu_tanh(x):
    c = math.sqrt(2.0 / math.pi)
    return x * (0.5 * (1.0 + jnp.tanh(c * (x + 0.044715 * (x * x * x)))))


def _lru_prep_kernel(h_ref, g_ref, w_ref, xb_ref, gate_ref):
    u = _rms(h_ref[...], g_ref[...]).astype(BF16)
    for c in range(LRU_WIDTH // 256):
        sl = slice(c * 256, (c + 1) * 256)
        xb_ref[:, sl] = _dot(u, w_ref[:, c * 256:(c + 1) * 256])
        gate = _dot(u, w_ref[:, LRU_WIDTH + c * 256:LRU_WIDTH + (c + 1) * 256])
        gate_ref[:, sl] = _gelu_tanh(gate).astype(BF16)


def _lru_prep(h, g, w):
    rows = h.shape[0]
    tm = ROW_TILE
    row = lambda i: (i, 0)
    return pl.pallas_call(
        _lru_prep_kernel,
        grid=(rows // tm,),
        in_specs=[pl.BlockSpec((tm, D_MODEL), row), _full(g.shape), _full(w.shape)],
        out_specs=[pl.BlockSpec((tm, LRU_WIDTH), row)] * 2,
        out_shape=[jax.ShapeDtypeStruct((rows, LRU_WIDTH), F32),
                   jax.ShapeDtypeStruct((rows, LRU_WIDTH), BF16)],
        compiler_params=_cparams(1),
        name="lru_prep",
    )(h, g, w)


def _softplus(x):
    return jnp.maximum(x, 0.0) + jnp.log(1.0 + jnp.exp(-jnp.abs(x)))


def _group_scan(a, u, carry, reverse):
    row = lax.broadcasted_iota(jnp.int32, a.shape, 0)
    for k in (1, 2, 4):
        shift = (8 - k) if reverse else k
        keep = (row < 8 - k) if reverse else (row >= k)
        a_s = pltpu.roll(a, shift, 0)
        u_s = pltpu.roll(u, shift, 0)
        u = jnp.where(keep, a * u_s + u, u)
        a = jnp.where(keep, a * a_s, a)
    return a * carry + u


def _lru_scan_kernel(xb_ref, gate_ref, cw_ref, cb_ref, wg_ref, bg_ref, lam_ref, o_ref,
                     xc_ref, y_ref, *, lp):
    ch = LRU_CHUNK
    n_chunks = lp // ch
    cw = cw_ref[...]
    cb = cb_ref[...]

    for c in range(n_chunks):
        t0 = c * ch
        lo = 8 if c == 0 else 0
        hi = ch - 8 if c == n_chunks - 1 else ch
        n = hi - lo
        acc = cb
        for j in range(4):
            x = xb_ref[t0 + lo + j - 2:t0 + lo + j - 2 + n, :]
            if c == 0:
                r = lax.broadcasted_iota(jnp.int32, x.shape, 0) + (lo + j - 2)
                x = jnp.where(r >= N_PAD, x, 0.0)
            acc = acc + cw[j:j + 1] * x
        xc_ref[t0 + lo:t0 + hi, :] = acc
        if c == 0:
            xc_ref[0:8, :] = jnp.zeros((8, xc_ref.shape[1]), F32)
        if c == n_chunks - 1:
            base = lp - 8
            tail = xb_ref[base:lp, :]
            prev = xb_ref[base - 8:base, :]
            r = lax.broadcasted_iota(jnp.int32, tail.shape, 0)
            x_m2 = jnp.where(r >= 2, pltpu.roll(tail, 2, 0), pltpu.roll(prev, 2, 0))
            x_m1 = jnp.where(r >= 1, pltpu.roll(tail, 1, 0), pltpu.roll(prev, 1, 0))
            x_p1 = jnp.where(r < 7, pltpu.roll(tail, 7, 0), 0.0)
            xc_ref[base:lp, :] = (cb + cw[0:1] * x_m2 + cw[1:2] * x_m1
                                  + cw[2:3] * tail + cw[3:4] * x_p1)

    def gates(t0, d):
        xc = xc_ref[pl.ds(t0, ch), :]
        xcb = xc.astype(BF16)
        r = jax.nn.sigmoid(_dot(xcb, wg_ref[d, 0]) + bg_ref[2 * d:2 * d + 1, :])
        i = jax.nn.sigmoid(_dot(xcb, wg_ref[d, 1]) + bg_ref[2 * d + 1:2 * d + 2, :])
        log_a = (-LRU_C) * r * _softplus(-lam_ref[d:d + 1, :])
        a = jnp.exp(log_a)
        x = 1.0 - a * a
        u = jnp.where(x > 0.0, x * lax.rsqrt(x), 0.0) * (i * xc)
        return a, u

    width = xc_ref.shape[1]

    def fwd(c, carry):
        t0 = pl.multiple_of(c * ch, 8)
        a, u = gates(t0, 0)
        r = lax.broadcasted_iota(jnp.int32, u.shape, 0) + t0
        u = jnp.where(r >= N_PAD, u, 0.0)
        for g in range(ch // 8):
            hs = _group_scan(a[8 * g:8 * g + 8], u[8 * g:8 * g + 8], carry, False)
            y_ref[pl.ds(t0 + 8 * g, 8), :] = hs
            carry = hs[7:8]
        return carry

    lax.fori_loop(0, n_chunks, fwd, jnp.zeros((1, width), F32))

    def bwd(ci, carry):
        t0 = pl.multiple_of((n_chunks - 1 - ci) * ch, 8)
        a, u = gates(t0, 1)
        for g in reversed(range(ch // 8)):
            hs = _group_scan(a[8 * g:8 * g + 8], u[8 * g:8 * g + 8], carry, True)
            rows = pl.ds(t0 + 8 * g, 8)
            o_ref[rows, :] = ((y_ref[rows, :] + hs) * gate_ref[rows, :].astype(F32)).astype(BF16)
            carry = hs[0:1]
        return carry

    lax.fori_loop(0, n_chunks, bwd, jnp.zeros((1, width), F32))


def _lru_scan(xb, gate, cw, cb, wg, bg, lam):
    b, lp, _ = xb.shape
    w = LRU_BLOCK_W
    blk = pl.BlockSpec((None, lp, w), lambda i, j: (i, 0, j))
    col = lambda r: pl.BlockSpec((r, w), lambda i, j: (0, j))
    return pl.pallas_call(
        functools.partial(_lru_scan_kernel, lp=lp),
        grid=(b, LRU_BLOCKS),
        in_specs=[blk, blk, col(4), col(1),
                  pl.BlockSpec((2, 2, None, w, w), lambda i, j: (0, 0, j, 0, 0)),
                  col(4), col(2)],
        out_specs=blk,
        out_shape=jax.ShapeDtypeStruct((b, lp, LRU_WIDTH), BF16),
        scratch_shapes=[pltpu.VMEM((lp, w), F32), pltpu.VMEM((lp, w), F32)],
        compiler_params=_cparams(2),
        name="lru_scan",
    )(xb, gate, cw, cb, wg, bg, lam)


def _post_kernel(o_ref, h_ref, wo_ref, g_ref, win_ref, wout_ref, out_ref):
    g = g_ref[...]
    x = h_ref[...] + _rms(_dot(o_ref[...], wo_ref[...]), g[1:2])
    xn = _rms(x, g[2:3]).astype(BF16)
    acc = None
    for c in range(FFN_HIDDEN // FFN_CHUNK):
        lo = c * FFN_CHUNK
        gg = _dot(xn, win_ref[:, lo:lo + FFN_CHUNK])
        uu = _dot(xn, win_ref[:, FFN_HIDDEN + lo:FFN_HIDDEN + lo + FFN_CHUNK])
        a = (gg * jax.nn.sigmoid(gg) * uu).astype(BF16)
        part = _dot(a, wout_ref[lo:lo + FFN_CHUNK, :])
        acc = part if acc is None else acc + part
    out_ref[...] = x + _rms(acc, g[3:4])


def _post(o, h, w_o, g, w_in, w_out):
    rows = h.shape[0]
    tm = ROW_TILE
    row = lambda i: (i, 0)
    const = lambda a: pl.BlockSpec(a.shape, lambda i: (0, 0), pipeline_mode=pl.Buffered(1))
    return pl.pallas_call(
        _post_kernel,
        grid=(rows // tm,),
        in_specs=[pl.BlockSpec((tm, o.shape[1]), row), pl.BlockSpec((tm, D_MODEL), row),
                  const(w_o), _full(g.shape), const(w_in), const(w_out)],
        out_specs=pl.BlockSpec((tm, D_MODEL), row),
        out_shape=jax.ShapeDtypeStruct((rows, D_MODEL), F32),
        compiler_params=_cparams(1),
        name="post_ffn",
    )(o, h, w_o, g, w_in, w_out)


def _rope_rows(lp):
    pos = jnp.arange(lp, dtype=F32) - float(N_PAD)
    inv_freq = ROPE_THETA ** (-jnp.arange(0, MLA_ROPE, 2, dtype=F32) / MLA_ROPE)
    ang = pos[:, None] * inv_freq[None, :]
    c, s = jnp.cos(ang), jnp.sin(ang)
    z32 = jnp.zeros_like(c)
    z64 = jnp.zeros((lp, 64), F32)
    return (jnp.concatenate([c, c, z64], axis=1),
            jnp.concatenate([-s, z32, z64], axis=1),
            jnp.concatenate([z32, s, z64], axis=1))


def kernel(x, meta_tokens, norm_g, mla_w_in, mla_q_norm, mla_kv_norm, mla_w_uq, mla_w_ukv, mla_w_o,
           diff_w_in, diff_lambda, diff_subln, diff_w_o, lru_w_in, lru_conv_w, lru_conv_b,
           lru_w_gates, lru_b_gates, lru_lambda, lru_w_o, ffn_w_in, ffn_w_out):
    b, seq, d = x.shape
    depth = norm_g.shape[0]
    lp = seq + ROW_OFF
    assert d == D_MODEL and seq % 128 == 0 and lp % LRU_CHUNK == 0 and lp % ROW_TILE == 0

    meta = jnp.broadcast_to(meta_tokens[None].astype(x.dtype), (b, N_META, d))
    h = jnp.concatenate([jnp.zeros((b, N_PAD, d), x.dtype), meta, x], axis=1).reshape(b * lp, d)
    cos, slo, shi = _rope_rows(lp)
    slopes = 2.0 ** (-8.0 * jnp.arange(1, DIFF_HEADS + 1, dtype=F32) / DIFF_HEADS)

    for i in range(depth):
        kind, j = i % 3, i // 3
        g = norm_g[i]
        if kind == 0:
            w_in = jnp.pad(mla_w_in[j], ((0, 0), (0, 64))).astype(BF16)
            uq = mla_w_uq[j].reshape(MLA_Q_RANK, MLA_HEADS, MLA_NOPE + MLA_ROPE)
            uq = jnp.pad(uq, ((0, 0), (0, 0), (0, MLA_QK - MLA_NOPE - MLA_ROPE)))
            uq = uq.reshape(MLA_Q_RANK, MLA_HEADS * MLA_QK).astype(BF16)
            ukv = mla_w_ukv[j].reshape(MLA_KV_RANK, MLA_HEADS, MLA_NOPE + MLA_V)
            uk = ukv[:, :, :MLA_NOPE].reshape(MLA_KV_RANK, MLA_HEADS * MLA_NOPE).astype(BF16)
            uv = ukv[:, :, MLA_NOPE:].reshape(MLA_KV_RANK, MLA_HEADS * MLA_V).astype(BF16)
            q, k, v = _mla_prep(h, g[0:1], w_in, mla_q_norm[j][None], mla_kv_norm[j][None],
                                uq, uk, uv, cos, slo, shi, lp)
            o = _mla_attn(q.reshape(b, lp, -1), k.reshape(b, lp, -1), v.reshape(b, lp, -1), seq)
            w_o = mla_w_o[j]
        elif kind == 1:
            lambda_init = 0.8 - 0.6 * math.exp(-0.3 * i)
            q, k, v = _diff_prep(h, g[0:1], diff_w_in[j].astype(BF16))
            o = _diff_attn(slopes, q.reshape(b, lp, -1), k.reshape(b, lp, -1), v.reshape(b, lp, -1),
                           diff_lambda[j], diff_subln[j][None], seq, lambda_init)
            w_o = diff_w_o[j]
        else:
            xb, gate = _lru_prep(h, g[0:1], lru_w_in[j].astype(BF16))
            o = _lru_scan(xb.reshape(b, lp, -1), gate.reshape(b, lp, -1), lru_conv_w[j],
                          lru_conv_b[j][None], lru_w_gates[j].astype(BF16),
                          lru_b_gates[j].reshape(4, LRU_WIDTH), lru_lambda[j])
            w_o = lru_w_o[j]
        h = _post(o.reshape(b * lp, -1), h, w_o.astype(BF16), g,
                  ffn_w_in[i].astype(BF16), ffn_w_out[i].astype(BF16))
    return h.reshape(b, lp, d)[:, ROW_OFF:]
```

```python
import functools
import math

import jax
import jax.numpy as jnp
import numpy as np
from jax import lax
from jax.experimental import pallas as pl
from jax.experimental.pallas import tpu as pltpu

F32 = jnp.float32
BF16 = jnp.bfloat16

D_MODEL = 1024
N_META = 16
ROW_OFF = 128
N_PAD = ROW_OFF - N_META
NORM_EPS = 1e-6
LOG2E = math.log2(math.e)
NEG_BIG = -1e30

MLA_HEADS = 16
MLA_Q_RANK = 384
MLA_KV_RANK = 256
MLA_NOPE = 128
MLA_ROPE = 64
MLA_V = 128
MLA_QK = 256
ROPE_THETA = 10000.0

DIFF_HEADS = 8
DIFF_HEAD_DIM = 64

LRU_WIDTH = 1536
LRU_BLOCKS = 6
LRU_BLOCK_W = 256
LRU_C = 8.0
LRU_CHUNK = 384
HALF = 128

FFN_HIDDEN = 2816

ROW_TILE = 384
POST_TILE = 512
VMEM_LIMIT = 56 * 1024 * 1024


def _cparams(n_axes):
    return pltpu.CompilerParams(dimension_semantics=("arbitrary",) * n_axes,
                                vmem_limit_bytes=VMEM_LIMIT)


def _rms(x, g):
    return x * lax.rsqrt(jnp.mean(x * x, axis=-1, keepdims=True) + NORM_EPS) * g


def _dot(a, b):
    return jnp.dot(a, b, preferred_element_type=F32)


def _dot_nt(a, b):
    return lax.dot_general(a, b, (((1,), (1,)), ((), ())), preferred_element_type=F32)


def _full(shape):
    nd = len(shape)
    return pl.BlockSpec(shape, lambda *_: (0,) * nd)


def _rope128(v, cos, sin_lo, sin_hi):
    return v * cos + pltpu.roll(v, 96, 1) * sin_lo + pltpu.roll(v, 32, 1) * sin_hi


def _mla_prep_kernel(h_ref, g_ref, win_ref, qg_ref, kvg_ref, wuq_ref, wuk_ref, wuv_ref,
                     cos_ref, slo_ref, shi_ref, q_ref, k_ref, v_ref, *, q_scale):
    u = _rms(h_ref[...], g_ref[...]).astype(BF16)
    t = _dot(u, win_ref[...])
    cq = _rms(t[:, :MLA_Q_RANK], qg_ref[...]).astype(BF16)
    ckv = _rms(t[:, MLA_Q_RANK:MLA_Q_RANK + MLA_KV_RANK], kvg_ref[...]).astype(BF16)
    cos, slo, shi = cos_ref[...], slo_ref[...], shi_ref[...]
    kr = _rope128(t[:, MLA_Q_RANK + MLA_KV_RANK:], cos, slo, shi).astype(BF16)
    for hp in range(MLA_HEADS // 2):
        kn = _dot(ckv, wuk_ref[:, hp * 256:(hp + 1) * 256])
        v_ref[:, hp * 256:(hp + 1) * 256] = _dot(ckv, wuv_ref[:, hp * 256:(hp + 1) * 256]).astype(BF16)
        for j in range(2):
            h = 2 * hp + j
            q = _dot(cq, wuq_ref[:, h * MLA_QK:(h + 1) * MLA_QK])
            q_ref[:, h * MLA_QK:h * MLA_QK + 128] = (q[:, :128] * q_scale).astype(BF16)
            q_ref[:, h * MLA_QK + 128:(h + 1) * MLA_QK] = (
                _rope128(q[:, 128:], cos, slo, shi) * q_scale).astype(BF16)
            k_ref[:, h * MLA_QK:h * MLA_QK + 128] = kn[:, j * 128:(j + 1) * 128].astype(BF16)
            k_ref[:, h * MLA_QK + 128:(h + 1) * MLA_QK] = kr


def _mla_prep(h, g, w_in, qg, kvg, wuq, wuk, wuv, cos, slo, shi, lp):
    rows = h.shape[0]
    tm = ROW_TILE
    tiles_per_seq = lp // tm
    row = lambda i: (i, 0)
    pos = lambda i: (i % tiles_per_seq, 0)
    q_scale = (MLA_NOPE + MLA_ROPE) ** -0.5 * LOG2E
    return pl.pallas_call(
        functools.partial(_mla_prep_kernel, q_scale=q_scale),
        grid=(rows // tm,),
        in_specs=[pl.BlockSpec((tm, D_MODEL), row), _full(g.shape), _full(w_in.shape),
                  _full(qg.shape), _full(kvg.shape), _full(wuq.shape), _full(wuk.shape),
                  _full(wuv.shape), pl.BlockSpec((tm, 128), pos), pl.BlockSpec((tm, 128), pos),
                  pl.BlockSpec((tm, 128), pos)],
        out_specs=[pl.BlockSpec((tm, MLA_HEADS * MLA_QK), row),
                   pl.BlockSpec((tm, MLA_HEADS * MLA_QK), row),
                   pl.BlockSpec((tm, MLA_HEADS * MLA_V), row)],
        out_shape=[jax.ShapeDtypeStruct((rows, MLA_HEADS * MLA_QK), BF16),
                   jax.ShapeDtypeStruct((rows, MLA_HEADS * MLA_QK), BF16),
                   jax.ShapeDtypeStruct((rows, MLA_HEADS * MLA_V), BF16)],
        compiler_params=_cparams(1),
        name="mla_prep",
    )(h, g, w_in, qg, kvg, wuq, wuk, wuv, cos, slo, shi)


def _v_ones(v_ref, k0, n):
    return jnp.concatenate([v_ref[pl.ds(k0, n), :], jnp.ones((n, v_ref.shape[1]), BF16)], axis=1)


def _query_tiles(q_tile, o_ref, seq, tq):
    o_ref[0:N_PAD, :] = jnp.zeros((N_PAD, o_ref.shape[1]), o_ref.dtype)
    q_tile(N_PAD, N_META + tq)

    def body(i, carry):
        q_tile(pl.multiple_of(ROW_OFF + i * tq, 128), tq)
        return carry

    lax.fori_loop(1, seq // tq, body, 0)


def _pick_tile(seq, largest):
    t = largest
    while seq % t:
        t //= 2
    return t


def _mla_attn_kernel(q_ref, k_ref, v_ref, o_ref, *, seq, tq, tk):
    dv = v_ref.shape[1]

    def q_tile(r0, rows):
        q = q_ref[pl.ds(r0, rows), :]
        s = _dot_nt(q, k_ref[0:ROW_OFF, :])
        col = lax.broadcasted_iota(jnp.int32, (rows, ROW_OFF), 1)
        s = jnp.where(col >= N_PAD, s, NEG_BIG)
        m = jnp.max(s, axis=1, keepdims=True)
        acc = _dot(jnp.exp2(s - m).astype(BF16), _v_ones(v_ref, 0, ROW_OFF))
        for c in range(seq // tk):
            k0 = ROW_OFF + c * tk
            s = _dot_nt(q, k_ref[k0:k0 + tk, :])
            m_new = jnp.maximum(m, jnp.max(s, axis=1, keepdims=True))
            p = jnp.exp2(s - m_new)
            acc = jnp.exp2(m - m_new) * acc + _dot(p.astype(BF16), _v_ones(v_ref, k0, tk))
            m = m_new
        o_ref[pl.ds(r0, rows), :] = (acc[:, :dv] / acc[:, dv:]).astype(BF16)

    _query_tiles(q_tile, o_ref, seq, tq)


def _mla_attn(q, k, v, seq):
    b, lp, _ = q.shape
    tq = _pick_tile(seq, 4096)
    tk = _pick_tile(seq, 256)
    return pl.pallas_call(
        functools.partial(_mla_attn_kernel, seq=seq, tq=tq, tk=tk),
        grid=(b, MLA_HEADS),
        in_specs=[pl.BlockSpec((None, lp, MLA_QK), lambda i, j: (i, 0, j)),
                  pl.BlockSpec((None, lp, MLA_QK), lambda i, j: (i, 0, j)),
                  pl.BlockSpec((None, lp, MLA_V), lambda i, j: (i, 0, j))],
        out_specs=pl.BlockSpec((None, lp, MLA_V), lambda i, j: (i, 0, j)),
        out_shape=jax.ShapeDtypeStruct((b, lp, MLA_HEADS * MLA_V), BF16),
        compiler_params=_cparams(2),
        name="mla_attn",
    )(q, k, v)


def _diff_prep_kernel(h_ref, g_ref, w_ref, q_ref, k_ref, v_ref, *, q_scale):
    u = _rms(h_ref[...], g_ref[...]).astype(BF16)
    n = q_ref.shape[1]
    for c in range(n // 256):
        sl = slice(c * 256, (c + 1) * 256)
        q_ref[:, sl] = (_dot(u, w_ref[:, c * 256:(c + 1) * 256]) * q_scale).astype(BF16)
        k_ref[:, sl] = _dot(u, w_ref[:, n + c * 256:n + (c + 1) * 256]).astype(BF16)
        v_ref[:, sl] = _dot(u, w_ref[:, 2 * n + c * 256:2 * n + (c + 1) * 256]).astype(BF16)


def _diff_prep(h, g, w):
    rows = h.shape[0]
    tm = ROW_TILE
    n = w.shape[1] // 3
    row = lambda i: (i, 0)
    return pl.pallas_call(
        functools.partial(_diff_prep_kernel, q_scale=DIFF_HEAD_DIM ** -0.5 * LOG2E),
        grid=(rows // tm,),
        in_specs=[pl.BlockSpec((tm, D_MODEL), row), _full(g.shape), _full(w.shape)],
        out_specs=[pl.BlockSpec((tm, n), row)] * 3,
        out_shape=[jax.ShapeDtypeStruct((rows, n), BF16)] * 3,
        compiler_params=_cparams(1),
        name="diff_prep",
    )(h, g, w)


BIAS_SPLIT = 32


def _split3(x):
    c1 = x.astype(BF16).astype(F32)
    c2 = (x - c1).astype(BF16).astype(F32)
    c3 = (x - c1 - c2).astype(BF16).astype(F32)
    return c1, c2, c3


def _diff_attn_kernel(slope_ref, q_ref, k_ref, v_ref, lam_ref, g_ref, o_ref, *,
                      seq, tq, tk, lambda_init):
    slope2 = slope_ref[pl.program_id(1)] * LOG2E
    lp4 = lam_ref[...]
    lam = (jnp.exp(jnp.sum(lp4[0:1] * lp4[1:2], axis=1, keepdims=True))
           - jnp.exp(jnp.sum(lp4[2:3] * lp4[3:4], axis=1, keepdims=True)) + lambda_init)
    g = g_ref[...]
    dv = v_ref.shape[1]
    n_chunks = seq // tk
    per_tile = tq // tk

    def lanes3(lane, x1, x2, x3):
        return jnp.where(lane % 3 == 0, x1, jnp.where(lane % 3 == 1, x2, x3))

    c1, c2, c3 = _split3(jnp.full((1, 128), slope2, F32))
    lane_k = lax.broadcasted_iota(jnp.int32, (tk, 128), 1)
    jj = lax.broadcasted_iota(jnp.int32, (tk, 128), 0).astype(F32)
    piece = lanes3(lane_k, c1, c2, c3)
    k_const = jnp.where(lane_k < 3, jj,
                        jnp.where(lane_k < 6, float(BIAS_SPLIT) * piece,
                                  jnp.where(lane_k < 9, piece, 0.0))).astype(BF16)

    def k_aug(k0, n, const):
        return jnp.concatenate([k_ref[pl.ds(k0, n), :], const], axis=1)

    def q_tile(r0, rows, t):
        first = isinstance(t, int)
        q = q_ref[pl.ds(r0, rows), :]
        lane = lax.broadcasted_iota(jnp.int32, q.shape, 1)
        zero = jnp.zeros_like(q)
        qs = (jnp.where(lane < DIFF_HEAD_DIM, q, zero), jnp.where(lane >= DIFF_HEAD_DIM, q, zero))
        ii = lax.broadcasted_iota(jnp.int32, (rows, 128), 0)
        if first:
            ii = ii - N_META
        lane_q = lax.broadcasted_iota(jnp.int32, (rows, 128), 1)
        in_a = (lane_q >= 3) & (lane_q < 6)
        base = jnp.where(lane_q < 3, lanes3(lane_q, c1, c2, c3),
                         jnp.where(in_a, -(ii >> 5).astype(F32),
                                   jnp.where(lane_q < 9, -(ii & (BIAS_SPLIT - 1)).astype(F32), 0.0))
                         ).astype(BF16)
        unit = jnp.where(in_a, -1.0, 0.0).astype(BF16)

        def q_aug(sgn, n_hi):
            tt = base + unit * (n_hi if isinstance(n_hi, float) else n_hi.astype(BF16))
            if not isinstance(sgn, float):
                tt = tt * sgn.astype(BF16)
            elif sgn != 1.0:
                tt = tt * sgn
            tt = tt.astype(BF16)
            return [jnp.concatenate([qs[c], tt], axis=1) for c in range(2)]

        def above_diag(width, off):
            d = (lax.broadcasted_iota(jnp.int32, (rows, width), 1)
                 - lax.broadcasted_iota(jnp.int32, (rows, width), 0))
            if first:
                d = d + N_META
            return jnp.maximum((d - off).astype(F32), 0.0) * (2.0 * slope2)

        base_row = ROW_OFF + t * tq
        n_meta = float(base_row // BIAS_SPLIT) if first else (base_row // BIAS_SPLIT).astype(F32)
        qa = q_aug(1.0, n_meta)
        kk = k_aug(0, ROW_OFF, k_const[0:ROW_OFF])
        col = lax.broadcasted_iota(jnp.int32, (rows, ROW_OFF), 1)
        vv = _v_ones(v_ref, 0, ROW_OFF)
        fix = above_diag(ROW_OFF, ROW_OFF) if first else None
        m, acc = [], []
        for c in range(2):
            s = _dot_nt(qa[c], kk)
            if first:
                s = s - fix
            s = jnp.where(col >= N_PAD, s, NEG_BIG)
            mc = jnp.max(s, axis=1, keepdims=True)
            m.append(mc)
            acc.append(_dot(jnp.exp2(s - mc).astype(BF16), vv))

        for rel in range(n_chunks):
            diag = rel < per_tile
            if first:
                sgn = 1.0 if diag else -1.0
                n_hi = float(-(rel * tk) // BIAS_SPLIT)
                k0 = ROW_OFF + rel * tk
            else:
                raw = t * per_tile + rel
                wrapped = raw >= n_chunks
                c_idx = jnp.where(wrapped, raw - n_chunks, raw)
                sgn = 1.0 if diag else jnp.where(wrapped, 1.0, -1.0)
                n_hi = ((t * per_tile - c_idx) * (tk // BIAS_SPLIT)).astype(F32)
                k0 = pl.multiple_of(ROW_OFF + c_idx * tk, 128)
            qa = q_aug(sgn, n_hi)
            kk = k_aug(k0, tk, k_const)
            vv = _v_ones(v_ref, k0, tk)
            fix = above_diag(tk, -rel * tk) if diag else None
            for c in range(2):
                s = _dot_nt(qa[c], kk)
                if diag:
                    s = s - fix
                m_new = jnp.maximum(m[c], jnp.max(s, axis=1, keepdims=True))
                p = jnp.exp2(s - m_new)
                acc[c] = jnp.exp2(m[c] - m_new) * acc[c] + _dot(p.astype(BF16), vv)
                m[c] = m_new
        o = acc[0][:, :dv] / acc[0][:, dv:] - lam * (acc[1][:, :dv] / acc[1][:, dv:])
        o = _rms(o, g) * (1.0 - lambda_init)
        o_ref[pl.ds(r0, rows), :] = o.astype(BF16)

    o_ref[0:N_PAD, :] = jnp.zeros((N_PAD, o_ref.shape[1]), o_ref.dtype)
    q_tile(N_PAD, N_META + tq, 0)

    def body(i, carry):
        q_tile(pl.multiple_of(ROW_OFF + i * tq, 128), tq, i)
        return carry

    lax.fori_loop(1, seq // tq, body, 0)


def _diff_attn(slopes, q, k, v, lam_p, g, seq, lambda_init):
    b, lp, _ = q.shape
    tq = _pick_tile(seq, 512)
    tk = _pick_tile(seq, 256)
    w = 2 * DIFF_HEAD_DIM
    blk = pl.BlockSpec((None, lp, w), lambda i, j: (i, 0, j))
    return pl.pallas_call(
        functools.partial(_diff_attn_kernel, seq=seq, tq=tq, tk=tk, lambda_init=lambda_init),
        grid=(b, DIFF_HEADS),
        in_specs=[pl.BlockSpec(memory_space=pltpu.SMEM), blk, blk, blk,
                  _full(lam_p.shape), _full(g.shape)],
        out_specs=blk,
        out_shape=jax.ShapeDtypeStruct((b, lp, DIFF_HEADS * w), BF16),
        compiler_params=_cparams(2),
        name="diff_attn",
    )(slopes, q, k, v, lam_p, g)


def _gelu_tanh(x):
    c = math.sqrt(2.0 / math.pi)
    return x * (0.5 * (1.0 + jnp.tanh(c * (x + 0.044715 * (x * x * x)))))


def _lru_prep_kernel(h_ref, g_ref, w_ref, xb_ref, gate_ref):
    u = _rms(h_ref[...], g_ref[...]).astype(BF16)
    for c in range(LRU_WIDTH // 256):
        sl = slice(c * 256, (c + 1) * 256)
        xb_ref[:, sl] = _dot(u, w_ref[:, c * 256:(c + 1) * 256])
        gate = _dot(u, w_ref[:, LRU_WIDTH + c * 256:LRU_WIDTH + (c + 1) * 256])
        gate_ref[:, sl] = _gelu_tanh(gate).astype(BF16)


def _lru_prep(h, g, w):
    rows = h.shape[0]
    tm = ROW_TILE
    row = lambda i: (i, 0)
    return pl.pallas_call(
        _lru_prep_kernel,
        grid=(rows // tm,),
        in_specs=[pl.BlockSpec((tm, D_MODEL), row), _full(g.shape), _full(w.shape)],
        out_specs=[pl.BlockSpec((tm, LRU_WIDTH), row)] * 2,
        out_shape=[jax.ShapeDtypeStruct((rows, LRU_WIDTH), F32),
                   jax.ShapeDtypeStruct((rows, LRU_WIDTH), BF16)],
        compiler_params=_cparams(1),
        name="lru_prep",
    )(h, g, w)


def _softplus(x):
    return jnp.maximum(x, 0.0) + jnp.log(1.0 + jnp.exp(-jnp.abs(x)))


def _group_scan(a, u, carry, reverse):
    row = lax.broadcasted_iota(jnp.int32, a.shape, 0)
    for k in (1, 2, 4):
        shift = (8 - k) if reverse else k
        keep = (row < 8 - k) if reverse else (row >= k)
        a_s = pltpu.roll(a, shift, 0)
        u_s = pltpu.roll(u, shift, 0)
        u = jnp.where(keep, a * u_s + u, u)
        a = jnp.where(keep, a * a_s, a)
    return a * carry + u


def _odd_part(n):
    while n % 2 == 0:
        n //= 2
    return n


def _lru_scan_kernel(xb_ref, gate_ref, cw_ref, cb_ref, wg_ref, bg_ref, lam_ref, o_ref,
                     xc_ref, y_ref, yb_ref, *, lp, seg):
    ch = LRU_CHUNK
    cw = cw_ref[...]
    cb = cb_ref[...]
    width = 2 * HALF

    def put(ref, rows, val):
        ref[0, rows, :] = val[:, :HALF]
        ref[1, rows, :] = val[:, HALF:]

    n_conv = lp // ch
    for c in range(n_conv):
        t0 = c * ch
        lo = 8 if c == 0 else 0
        hi = ch - 8 if c == n_conv - 1 else ch
        n = hi - lo
        acc = cb
        for j in range(4):
            x = xb_ref[t0 + lo + j - 2:t0 + lo + j - 2 + n, :]
            if c == 0:
                r = lax.broadcasted_iota(jnp.int32, x.shape, 0) + (lo + j - 2)
                x = jnp.where(r >= N_PAD, x, 0.0)
            acc = acc + cw[j:j + 1] * x
        put(xc_ref, slice(t0 + lo, t0 + hi), acc)
        if c == n_conv - 1:
            base = lp - 8
            tail = xb_ref[base:lp, :]
            prev = xb_ref[base - 8:base, :]
            r = lax.broadcasted_iota(jnp.int32, tail.shape, 0)
            x_m2 = jnp.where(r >= 2, pltpu.roll(tail, 2, 0), pltpu.roll(prev, 2, 0))
            x_m1 = jnp.where(r >= 1, pltpu.roll(tail, 1, 0), pltpu.roll(prev, 1, 0))
            x_p1 = jnp.where(r < 7, pltpu.roll(tail, 7, 0), 0.0)
            put(xc_ref, slice(base, lp), cb + cw[0:1] * x_m2 + cw[1:2] * x_m1 + cw[2:3] * tail + cw[3:4] * x_p1)
    put(xc_ref, slice(0, N_PAD), jnp.zeros((N_PAD, width), F32))

    sc = 8 * seg
    n_chunks = lp // sc

    def strided(t0, j):
        return pl.ds(t0 + j, 8, stride=seg)

    def load_perm(ref, t0):
        halves = [jnp.concatenate([ref[hh, strided(t0, j), :] for j in range(seg)], axis=0) for hh in range(2)]
        return jnp.concatenate(halves, axis=1)

    def gates(xc, d, rate):
        xcb = xc.astype(BF16)
        r = jax.nn.sigmoid(_dot(xcb, wg_ref[d, 0]) + bg_ref[2 * d:2 * d + 1, :])
        i = jax.nn.sigmoid(_dot(xcb, wg_ref[d, 1]) + bg_ref[2 * d + 1:2 * d + 2, :])
        a = jnp.exp2(r * rate)
        x = 1.0 - a * a
        u = jnp.where(x > 0.0, x * lax.rsqrt(x), 0.0) * (i * xc)
        return a, u

    def scan_chunk(a, u, carry, reverse):
        hs, ps = [None] * seg, [None] * seg
        h = p = None
        for j in (reversed(range(seg)) if reverse else range(seg)):
            aj, uj = a[8 * j:8 * j + 8], u[8 * j:8 * j + 8]
            h = uj if h is None else aj * h + uj
            p = aj if p is None else aj * p
            hs[j], ps[j] = h, p
        ends = _group_scan(p, h, carry, reverse)
        row = lax.broadcasted_iota(jnp.int32, ends.shape, 0)
        if reverse:
            entering = jnp.where(row == 7, carry, pltpu.roll(ends, 7, 0))
            carry = ends[0:1]
        else:
            entering = jnp.where(row == 0, carry, pltpu.roll(ends, 1, 0))
            carry = ends[7:8]
        return [hs[j] + ps[j] * entering for j in range(seg)], carry

    def rate(d):
        return _softplus(-lam_ref[d:d + 1, :]) * (-LRU_C * LOG2E)

    rates = (rate(0), rate(1))

    def put_perm(ref, t0, ys):
        for j in range(seg):
            ref[0, strided(t0, j), :] = ys[j][:, :HALF]
            ref[1, strided(t0, j), :] = ys[j][:, HALF:]

    def both(c, carries):
        cf, cbk = carries
        tf = pl.multiple_of(c * sc, 8)
        tb = pl.multiple_of((n_chunks - 1 - c) * sc, 8)
        af, uf = gates(load_perm(xc_ref, tf), 0, rates[0])
        ab, ub = gates(load_perm(xc_ref, tb), 1, rates[1])
        yf, cf = scan_chunk(af, uf, cf, False)
        yb, cbk = scan_chunk(ab, ub, cbk, True)
        put_perm(y_ref, tf, yf)
        put_perm(yb_ref, tb, yb)
        return cf, cbk

    zero = jnp.zeros((1, width), F32)
    lax.fori_loop(0, n_chunks, both, (zero, zero))

    def finish(c, carry):
        rows = pl.ds(pl.multiple_of(c * ch, 128), ch)
        y = jnp.concatenate([y_ref[0, rows, :] + yb_ref[0, rows, :],
                             y_ref[1, rows, :] + yb_ref[1, rows, :]], axis=1)
        o_ref[rows, :] = (y * gate_ref[rows, :].astype(F32)).astype(BF16)
        return carry

    lax.fori_loop(0, lp // ch, finish, 0)


def _lru_scan(xb, gate, cw, cb, wg, bg, lam):
    b, lp, _ = xb.shape
    w = LRU_BLOCK_W
    seg = _odd_part(lp // 8)
    blk = pl.BlockSpec((None, lp, w), lambda i, j: (i, 0, j))
    col = lambda r: pl.BlockSpec((r, w), lambda i, j: (0, j))
    return pl.pallas_call(
        functools.partial(_lru_scan_kernel, lp=lp, seg=seg),
        grid=(b, LRU_BLOCKS),
        in_specs=[blk, blk, col(4), col(1),
                  pl.BlockSpec((2, 2, None, w, w), lambda i, j: (0, 0, j, 0, 0)),
                  col(4), col(2)],
        out_specs=blk,
        out_shape=jax.ShapeDtypeStruct((b, lp, LRU_WIDTH), BF16),
        scratch_shapes=[pltpu.VMEM((2, lp, HALF), F32)] * 3,
        compiler_params=_cparams(2),
        name="lru_scan",
    )(xb, gate, cw, cb, wg, bg, lam)


def _post_kernel(o_ref, h_ref, wo_ref, g_ref, win_ref, wout_ref, out_ref):
    g = g_ref[...]
    x = h_ref[...] + _rms(_dot(o_ref[...], wo_ref[...]), g[1:2])
    xn = _rms(x, g[2:3]).astype(BF16)
    gg = _dot(xn, win_ref[:, :FFN_HIDDEN])
    uu = _dot(xn, win_ref[:, FFN_HIDDEN:])
    a = (gg * jax.nn.sigmoid(gg) * uu).astype(BF16)
    out_ref[...] = x + _rms(_dot(a, wout_ref[...]), g[3:4])


def _post(o, h, w_o, g, w_in, w_out):
    rows = h.shape[0]
    tm = POST_TILE if rows % POST_TILE == 0 else ROW_TILE
    row = lambda i: (i, 0)
    const = lambda a: pl.BlockSpec(a.shape, lambda i: (0, 0), pipeline_mode=pl.Buffered(1))
    return pl.pallas_call(
        _post_kernel,
        grid=(rows // tm,),
        in_specs=[pl.BlockSpec((tm, o.shape[1]), row), pl.BlockSpec((tm, D_MODEL), row),
                  const(w_o), _full(g.shape), const(w_in), const(w_out)],
        out_specs=pl.BlockSpec((tm, D_MODEL), row),
        out_shape=jax.ShapeDtypeStruct((rows, D_MODEL), F32),
        compiler_params=_cparams(1),
        name="post_ffn",
    )(o, h, w_o, g, w_in, w_out)


def _rope_rows(lp):
    pos = jnp.arange(lp, dtype=F32) - float(N_PAD)
    inv_freq = ROPE_THETA ** (-jnp.arange(0, MLA_ROPE, 2, dtype=F32) / MLA_ROPE)
    ang = pos[:, None] * inv_freq[None, :]
    c, s = jnp.cos(ang), jnp.sin(ang)
    z32 = jnp.zeros_like(c)
    z64 = jnp.zeros((lp, 64), F32)
    return (jnp.concatenate([c, c, z64], axis=1),
            jnp.concatenate([-s, z32, z64], axis=1),
            jnp.concatenate([z32, s, z64], axis=1))


def kernel(x, meta_tokens, norm_g, mla_w_in, mla_q_norm, mla_kv_norm, mla_w_uq, mla_w_ukv, mla_w_o,
           diff_w_in, diff_lambda, diff_subln, diff_w_o, lru_w_in, lru_conv_w, lru_conv_b,
           lru_w_gates, lru_b_gates, lru_lambda, lru_w_o, ffn_w_in, ffn_w_out):
    b, seq, d = x.shape
    depth = norm_g.shape[0]
    lp = seq + ROW_OFF
    assert d == D_MODEL and seq % 128 == 0 and lp % LRU_CHUNK == 0 and lp % ROW_TILE == 0

    meta = jnp.broadcast_to(meta_tokens[None].astype(x.dtype), (b, N_META, d))
    h = jnp.concatenate([jnp.zeros((b, N_PAD, d), x.dtype), meta, x], axis=1).reshape(b * lp, d)
    cos, slo, shi = _rope_rows(lp)
    slopes = 2.0 ** (-8.0 * jnp.arange(1, DIFF_HEADS + 1, dtype=F32) / DIFF_HEADS)

    for i in range(depth):
        kind, j = i % 3, i // 3
        g = norm_g[i]
        if kind == 0:
            w_in = jnp.pad(mla_w_in[j], ((0, 0), (0, 64))).astype(BF16)
            uq = mla_w_uq[j].reshape(MLA_Q_RANK, MLA_HEADS, MLA_NOPE + MLA_ROPE)
            uq = jnp.pad(uq, ((0, 0), (0, 0), (0, MLA_QK - MLA_NOPE - MLA_ROPE)))
            uq = uq.reshape(MLA_Q_RANK, MLA_HEADS * MLA_QK).astype(BF16)
            ukv = mla_w_ukv[j].reshape(MLA_KV_RANK, MLA_HEADS, MLA_NOPE + MLA_V)
            uk = ukv[:, :, :MLA_NOPE].reshape(MLA_KV_RANK, MLA_HEADS * MLA_NOPE).astype(BF16)
            uv = ukv[:, :, MLA_NOPE:].reshape(MLA_KV_RANK, MLA_HEADS * MLA_V).astype(BF16)
            q, k, v = _mla_prep(h, g[0:1], w_in, mla_q_norm[j][None], mla_kv_norm[j][None],
                                uq, uk, uv, cos, slo, shi, lp)
            o = _mla_attn(q.reshape(b, lp, -1), k.reshape(b, lp, -1), v.reshape(b, lp, -1), seq)
            w_o = mla_w_o[j]
        elif kind == 1:
            lambda_init = 0.8 - 0.6 * math.exp(-0.3 * i)
            q, k, v = _diff_prep(h, g[0:1], diff_w_in[j].astype(BF16))
            o = _diff_attn(slopes, q.reshape(b, lp, -1), k.reshape(b, lp, -1), v.reshape(b, lp, -1),
                           diff_lambda[j], diff_subln[j][None], seq, lambda_init)
            w_o = diff_w_o[j]
        else:
            xb, gate = _lru_prep(h, g[0:1], lru_w_in[j].astype(BF16))
            o = _lru_scan(xb.reshape(b, lp, -1), gate.reshape(b, lp, -1), lru_conv_w[j],
                          lru_conv_b[j][None], lru_w_gates[j].astype(BF16),
                          lru_b_gates[j].reshape(4, LRU_WIDTH), lru_lambda[j])
            w_o = lru_w_o[j]
        h = _post(o.reshape(b * lp, -1), h, w_o.astype(BF16), g,
                  ffn_w_in[i].astype(BF16), ffn_w_out[i].astype(BF16))
    return h.reshape(b, lp, d)[:, ROW_OFF:]
```

```python
import functools
import math

import jax
import jax.numpy as jnp
import numpy as np
from jax import lax
from jax.experimental import pallas as pl
from jax.experimental.pallas import tpu as pltpu

F32 = jnp.float32
BF16 = jnp.bfloat16

D_MODEL = 1024
N_META = 16
ROW_OFF = 128
N_PAD = ROW_OFF - N_META
NORM_EPS = 1e-6
LOG2E = math.log2(math.e)
NEG_BIG = -1e30

MLA_HEADS = 16
MLA_Q_RANK = 384
MLA_KV_RANK = 256
MLA_NOPE = 128
MLA_ROPE = 64
MLA_V = 128
MLA_QK = 256
ROPE_THETA = 10000.0

DIFF_HEADS = 8
DIFF_HEAD_DIM = 64

LRU_WIDTH = 1536
LRU_BLOCKS = 6
LRU_BLOCK_W = 256
LRU_C = 8.0
LRU_CHUNK = 384
HALF = 128

FFN_HIDDEN = 2816

ROW_TILE = 384
POST_TILE = 512
VMEM_LIMIT = 56 * 1024 * 1024


def _cparams(n_axes):
    return pltpu.CompilerParams(dimension_semantics=("arbitrary",) * n_axes,
                                vmem_limit_bytes=VMEM_LIMIT)


def _rms(x, g):
    return x * lax.rsqrt(jnp.mean(x * x, axis=-1, keepdims=True) + NORM_EPS) * g


def _dot(a, b):
    return jnp.dot(a, b, preferred_element_type=F32)


def _dot_nt(a, b):
    return lax.dot_general(a, b, (((1,), (1,)), ((), ())), preferred_element_type=F32)


def _full(shape):
    nd = len(shape)
    return pl.BlockSpec(shape, lambda *_: (0,) * nd)


def _layer(a, i, **kw):
    nd = a.ndim - 1
    return pl.BlockSpec((None,) + a.shape[1:], lambda *_: (i,) + (0,) * nd, **kw)


def _rope128(v, cos, sin_lo, sin_hi):
    return v * cos + pltpu.roll(v, 96, 1) * sin_lo + pltpu.roll(v, 32, 1) * sin_hi


def _mla_prep_kernel(h_ref, g_ref, win_ref, qg_ref, kvg_ref, wuq_ref, wuk_ref, wuv_ref,
                     cos_ref, slo_ref, shi_ref, q_ref, k_ref, v_ref, *, q_scale):
    u = _rms(h_ref[...], g_ref[...]).astype(BF16)
    t = _dot(u, win_ref[...])
    cq = _rms(t[:, :MLA_Q_RANK], qg_ref[...]).astype(BF16)
    ckv = _rms(t[:, MLA_Q_RANK:MLA_Q_RANK + MLA_KV_RANK], kvg_ref[...]).astype(BF16)
    cos, slo, shi = cos_ref[...], slo_ref[...], shi_ref[...]
    kr = _rope128(t[:, MLA_Q_RANK + MLA_KV_RANK:], cos, slo, shi).astype(BF16)
    for hp in range(MLA_HEADS // 2):
        kn = _dot(ckv, wuk_ref[:, hp * 256:(hp + 1) * 256])
        v_ref[:, hp * 256:(hp + 1) * 256] = _dot(ckv, wuv_ref[:, hp * 256:(hp + 1) * 256]).astype(BF16)
        for j in range(2):
            h = 2 * hp + j
            q = _dot(cq, wuq_ref[:, h * MLA_QK:(h + 1) * MLA_QK])
            q_ref[:, h * MLA_QK:h * MLA_QK + 128] = (q[:, :128] * q_scale).astype(BF16)
            q_ref[:, h * MLA_QK + 128:(h + 1) * MLA_QK] = (
                _rope128(q[:, 128:], cos, slo, shi) * q_scale).astype(BF16)
            k_ref[:, h * MLA_QK:h * MLA_QK + 128] = kn[:, j * 128:(j + 1) * 128].astype(BF16)
            k_ref[:, h * MLA_QK + 128:(h + 1) * MLA_QK] = kr


def _mla_prep(h, g, j, w_in, qg, kvg, wuq, wuk, wuv, cos, slo, shi, lp):
    rows = h.shape[0]
    tm = ROW_TILE
    tiles_per_seq = lp // tm
    row = lambda i: (i, 0)
    pos = lambda i: (i % tiles_per_seq, 0)
    q_scale = (MLA_NOPE + MLA_ROPE) ** -0.5 * LOG2E
    return pl.pallas_call(
        functools.partial(_mla_prep_kernel, q_scale=q_scale),
        grid=(rows // tm,),
        in_specs=[pl.BlockSpec((tm, D_MODEL), row), _full(g.shape), _layer(w_in, j),
                  _layer(qg, j), _layer(kvg, j), _layer(wuq, j), _layer(wuk, j),
                  _layer(wuv, j), pl.BlockSpec((tm, 128), pos), pl.BlockSpec((tm, 128), pos),
                  pl.BlockSpec((tm, 128), pos)],
        out_specs=[pl.BlockSpec((tm, MLA_HEADS * MLA_QK), row),
                   pl.BlockSpec((tm, MLA_HEADS * MLA_QK), row),
                   pl.BlockSpec((tm, MLA_HEADS * MLA_V), row)],
        out_shape=[jax.ShapeDtypeStruct((rows, MLA_HEADS * MLA_QK), BF16),
                   jax.ShapeDtypeStruct((rows, MLA_HEADS * MLA_QK), BF16),
                   jax.ShapeDtypeStruct((rows, MLA_HEADS * MLA_V), BF16)],
        compiler_params=_cparams(1),
        name="mla_prep",
    )(h, g, w_in, qg, kvg, wuq, wuk, wuv, cos, slo, shi)


def _v_ones(v_ref, k0, n):
    return jnp.concatenate([v_ref[pl.ds(k0, n), :], jnp.ones((n, v_ref.shape[1]), BF16)], axis=1)


def _query_tiles(q_tile, o_ref, seq, tq):
    o_ref[0:N_PAD, :] = jnp.zeros((N_PAD, o_ref.shape[1]), o_ref.dtype)
    q_tile(N_PAD, N_META + tq)

    def body(i, carry):
        q_tile(pl.multiple_of(ROW_OFF + i * tq, 128), tq)
        return carry

    lax.fori_loop(1, seq // tq, body, 0)


def _pick_tile(seq, largest):
    t = largest
    while seq % t:
        t //= 2
    return t


def _mla_attn_kernel(q_ref, k_ref, v_ref, o_ref, *, seq, tq, tk):
    dv = v_ref.shape[1]

    def q_tile(r0, rows):
        q = q_ref[pl.ds(r0, rows), :]
        s = _dot_nt(q, k_ref[0:ROW_OFF, :])
        col = lax.broadcasted_iota(jnp.int32, (rows, ROW_OFF), 1)
        s = jnp.where(col >= N_PAD, s, NEG_BIG)
        m = jnp.max(s, axis=1, keepdims=True)
        acc = _dot(jnp.exp2(s - m).astype(BF16), _v_ones(v_ref, 0, ROW_OFF))
        for c in range(seq // tk):
            k0 = ROW_OFF + c * tk
            s = _dot_nt(q, k_ref[k0:k0 + tk, :])
            m_new = jnp.maximum(m, jnp.max(s, axis=1, keepdims=True))
            p = jnp.exp2(s - m_new)
            acc = jnp.exp2(m - m_new) * acc + _dot(p.astype(BF16), _v_ones(v_ref, k0, tk))
            m = m_new
        o_ref[pl.ds(r0, rows), :] = (acc[:, :dv] / acc[:, dv:]).astype(BF16)

    _query_tiles(q_tile, o_ref, seq, tq)


def _mla_attn(q, k, v, seq):
    b, lp, _ = q.shape
    tq = _pick_tile(seq, 4096)
    tk = _pick_tile(seq, 256)
    return pl.pallas_call(
        functools.partial(_mla_attn_kernel, seq=seq, tq=tq, tk=tk),
        grid=(b, MLA_HEADS),
        in_specs=[pl.BlockSpec((None, lp, MLA_QK), lambda i, j: (i, 0, j)),
                  pl.BlockSpec((None, lp, MLA_QK), lambda i, j: (i, 0, j)),
                  pl.BlockSpec((None, lp, MLA_V), lambda i, j: (i, 0, j))],
        out_specs=pl.BlockSpec((None, lp, MLA_V), lambda i, j: (i, 0, j)),
        out_shape=jax.ShapeDtypeStruct((b, lp, MLA_HEADS * MLA_V), BF16),
        compiler_params=_cparams(2),
        name="mla_attn",
    )(q, k, v)


def _diff_prep_kernel(h_ref, g_ref, w_ref, q_ref, k_ref, v_ref, *, q_scale):
    u = _rms(h_ref[...], g_ref[...]).astype(BF16)
    n = q_ref.shape[1]
    for c in range(n // 256):
        sl = slice(c * 256, (c + 1) * 256)
        q_ref[:, sl] = (_dot(u, w_ref[:, c * 256:(c + 1) * 256]) * q_scale).astype(BF16)
        k_ref[:, sl] = _dot(u, w_ref[:, n + c * 256:n + (c + 1) * 256]).astype(BF16)
        v_ref[:, sl] = _dot(u, w_ref[:, 2 * n + c * 256:2 * n + (c + 1) * 256]).astype(BF16)


def _diff_prep(h, g, w, j):
    rows = h.shape[0]
    tm = ROW_TILE
    n = w.shape[2] // 3
    row = lambda i: (i, 0)
    return pl.pallas_call(
        functools.partial(_diff_prep_kernel, q_scale=DIFF_HEAD_DIM ** -0.5 * LOG2E),
        grid=(rows // tm,),
        in_specs=[pl.BlockSpec((tm, D_MODEL), row), _full(g.shape), _layer(w, j)],
        out_specs=[pl.BlockSpec((tm, n), row)] * 3,
        out_shape=[jax.ShapeDtypeStruct((rows, n), BF16)] * 3,
        compiler_params=_cparams(1),
        name="diff_prep",
    )(h, g, w)


BIAS_SPLIT = 32


def _split3(x):
    c1 = x.astype(BF16).astype(F32)
    c2 = (x - c1).astype(BF16).astype(F32)
    c3 = (x - c1 - c2).astype(BF16).astype(F32)
    return c1, c2, c3


def _diff_attn_kernel(slope_ref, q_ref, k_ref, v_ref, lam_ref, g_ref, o_ref, *,
                      seq, tq, tk, lambda_init):
    slope2 = slope_ref[pl.program_id(1)] * LOG2E
    lp4 = lam_ref[...]
    lam = (jnp.exp(jnp.sum(lp4[0:1] * lp4[1:2], axis=1, keepdims=True))
           - jnp.exp(jnp.sum(lp4[2:3] * lp4[3:4], axis=1, keepdims=True)) + lambda_init)
    g = g_ref[...]
    dv = v_ref.shape[1]
    n_chunks = seq // tk
    per_tile = tq // tk

    def lanes3(lane, x1, x2, x3):
        return jnp.where(lane % 3 == 0, x1, jnp.where(lane % 3 == 1, x2, x3))

    c1, c2, c3 = _split3(jnp.full((1, 128), slope2, F32))
    lane_k = lax.broadcasted_iota(jnp.int32, (tk, 128), 1)
    jj = lax.broadcasted_iota(jnp.int32, (tk, 128), 0).astype(F32)
    piece = lanes3(lane_k, c1, c2, c3)
    k_const = jnp.where(lane_k < 3, jj,
                        jnp.where(lane_k < 6, float(BIAS_SPLIT) * piece,
                                  jnp.where(lane_k < 9, piece, 0.0))).astype(BF16)

    def k_aug(k0, n, const):
        return jnp.concatenate([k_ref[pl.ds(k0, n), :], const], axis=1)

    def q_tile(r0, rows, t):
        first = isinstance(t, int)
        q = q_ref[pl.ds(r0, rows), :]
        lane = lax.broadcasted_iota(jnp.int32, q.shape, 1)
        zero = jnp.zeros_like(q)
        qs = (jnp.where(lane < DIFF_HEAD_DIM, q, zero), jnp.where(lane >= DIFF_HEAD_DIM, q, zero))
        ii = lax.broadcasted_iota(jnp.int32, (rows, 128), 0)
        if first:
            ii = ii - N_META
        lane_q = lax.broadcasted_iota(jnp.int32, (rows, 128), 1)
        in_a = (lane_q >= 3) & (lane_q < 6)
        base = jnp.where(lane_q < 3, lanes3(lane_q, c1, c2, c3),
                         jnp.where(in_a, -(ii >> 5).astype(F32),
                                   jnp.where(lane_q < 9, -(ii & (BIAS_SPLIT - 1)).astype(F32), 0.0))
                         ).astype(BF16)
        unit = jnp.where(in_a, -1.0, 0.0).astype(BF16)

        def q_aug(sgn, n_hi):
            tt = base + unit * (n_hi if isinstance(n_hi, float) else n_hi.astype(BF16))
            if not isinstance(sgn, float):
                tt = tt * sgn.astype(BF16)
            elif sgn != 1.0:
                tt = tt * sgn
            tt = tt.astype(BF16)
            return [jnp.concatenate([qs[c], tt], axis=1) for c in range(2)]

        def above_diag(width, off):
            d = (lax.broadcasted_iota(jnp.int32, (rows, width), 1)
                 - lax.broadcasted_iota(jnp.int32, (rows, width), 0))
            if first:
                d = d + N_META
            return jnp.maximum((d - off).astype(F32), 0.0) * (2.0 * slope2)

        base_row = ROW_OFF + t * tq
        n_meta = float(base_row // BIAS_SPLIT) if first else (base_row // BIAS_SPLIT).astype(F32)
        qa = q_aug(1.0, n_meta)
        kk = k_aug(0, ROW_OFF, k_const[0:ROW_OFF])
        col = lax.broadcasted_iota(jnp.int32, (rows, ROW_OFF), 1)
        vv = _v_ones(v_ref, 0, ROW_OFF)
        fix = above_diag(ROW_OFF, ROW_OFF) if first else None
        m, acc = [], []
        for c in range(2):
            s = _dot_nt(qa[c], kk)
            if first:
                s = s - fix
            s = jnp.where(col >= N_PAD, s, NEG_BIG)
            mc = jnp.max(s, axis=1, keepdims=True)
            m.append(mc)
            acc.append(_dot(jnp.exp2(s - mc).astype(BF16), vv))

        for rel in range(n_chunks):
            diag = rel < per_tile
            if first:
                sgn = 1.0 if diag else -1.0
                n_hi = float(-(rel * tk) // BIAS_SPLIT)
                k0 = ROW_OFF + rel * tk
            else:
                raw = t * per_tile + rel
                wrapped = raw >= n_chunks
                c_idx = jnp.where(wrapped, raw - n_chunks, raw)
                sgn = 1.0 if diag else jnp.where(wrapped, 1.0, -1.0)
                n_hi = ((t * per_tile - c_idx) * (tk // BIAS_SPLIT)).astype(F32)
                k0 = pl.multiple_of(ROW_OFF + c_idx * tk, 128)
            qa = q_aug(sgn, n_hi)
            kk = k_aug(k0, tk, k_const)
            vv = _v_ones(v_ref, k0, tk)
            fix = above_diag(tk, -rel * tk) if diag else None
            for c in range(2):
                s = _dot_nt(qa[c], kk)
                if diag:
                    s = s - fix
                m_new = jnp.maximum(m[c], jnp.max(s, axis=1, keepdims=True))
                p = jnp.exp2(s - m_new)
                acc[c] = jnp.exp2(m[c] - m_new) * acc[c] + _dot(p.astype(BF16), vv)
                m[c] = m_new
        o = acc[0][:, :dv] / acc[0][:, dv:] - lam * (acc[1][:, :dv] / acc[1][:, dv:])
        o = _rms(o, g) * (1.0 - lambda_init)
        o_ref[pl.ds(r0, rows), :] = o.astype(BF16)

    o_ref[0:N_PAD, :] = jnp.zeros((N_PAD, o_ref.shape[1]), o_ref.dtype)
    q_tile(N_PAD, N_META + tq, 0)

    def body(i, carry):
        q_tile(pl.multiple_of(ROW_OFF + i * tq, 128), tq, i)
        return carry

    lax.fori_loop(1, seq // tq, body, 0)


def _diff_attn(slopes, q, k, v, lam_p, g, seq, lambda_init):
    b, lp, _ = q.shape
    tq = _pick_tile(seq, 512)
    tk = _pick_tile(seq, 256)
    w = 2 * DIFF_HEAD_DIM
    blk = pl.BlockSpec((None, lp, w), lambda i, j: (i, 0, j))
    return pl.pallas_call(
        functools.partial(_diff_attn_kernel, seq=seq, tq=tq, tk=tk, lambda_init=lambda_init),
        grid=(b, DIFF_HEADS),
        in_specs=[pl.BlockSpec(memory_space=pltpu.SMEM), blk, blk, blk,
                  _full(lam_p.shape), _full(g.shape)],
        out_specs=blk,
        out_shape=jax.ShapeDtypeStruct((b, lp, DIFF_HEADS * w), BF16),
        compiler_params=_cparams(2),
        name="diff_attn",
    )(slopes, q, k, v, lam_p, g)


def _gelu_tanh(x):
    c = math.sqrt(2.0 / math.pi)
    return x * (0.5 * (1.0 + jnp.tanh(c * (x + 0.044715 * (x * x * x)))))


def _lru_prep_kernel(h_ref, g_ref, w_ref, xb_ref, gate_ref):
    u = _rms(h_ref[...], g_ref[...]).astype(BF16)
    for c in range(LRU_WIDTH // 256):
        sl = slice(c * 256, (c + 1) * 256)
        xb_ref[:, sl] = _dot(u, w_ref[:, c * 256:(c + 1) * 256])
        gate = _dot(u, w_ref[:, LRU_WIDTH + c * 256:LRU_WIDTH + (c + 1) * 256])
        gate_ref[:, sl] = _gelu_tanh(gate).astype(BF16)


def _lru_prep(h, g, w, j):
    rows = h.shape[0]
    tm = ROW_TILE
    row = lambda i: (i, 0)
    return pl.pallas_call(
        _lru_prep_kernel,
        grid=(rows // tm,),
        in_specs=[pl.BlockSpec((tm, D_MODEL), row), _full(g.shape), _layer(w, j)],
        out_specs=[pl.BlockSpec((tm, LRU_WIDTH), row)] * 2,
        out_shape=[jax.ShapeDtypeStruct((rows, LRU_WIDTH), F32),
                   jax.ShapeDtypeStruct((rows, LRU_WIDTH), BF16)],
        compiler_params=_cparams(1),
        name="lru_prep",
    )(h, g, w)


def _softplus(x):
    return jnp.maximum(x, 0.0) + jnp.log(1.0 + jnp.exp(-jnp.abs(x)))


def _group_scan(a, u, carry, reverse):
    row = lax.broadcasted_iota(jnp.int32, a.shape, 0)
    for k in (1, 2, 4):
        shift = (8 - k) if reverse else k
        keep = (row < 8 - k) if reverse else (row >= k)
        a_s = pltpu.roll(a, shift, 0)
        u_s = pltpu.roll(u, shift, 0)
        u = jnp.where(keep, a * u_s + u, u)
        a = jnp.where(keep, a * a_s, a)
    return a * carry + u


def _odd_part(n):
    while n % 2 == 0:
        n //= 2
    return n


def _lru_scan_kernel(xb_ref, gate_ref, cw_ref, cb_ref, wg_ref, bg_ref, lam_ref, o_ref,
                     xc_ref, y_ref, yb_ref, *, lp, seg):
    ch = LRU_CHUNK
    cw = cw_ref[...]
    cb = cb_ref[...]
    width = 2 * HALF

    def put(ref, rows, val):
        ref[0, rows, :] = val[:, :HALF]
        ref[1, rows, :] = val[:, HALF:]

    n_conv = lp // ch
    for c in range(n_conv):
        t0 = c * ch
        lo = 8 if c == 0 else 0
        hi = ch - 8 if c == n_conv - 1 else ch
        n = hi - lo
        acc = cb
        for j in range(4):
            x = xb_ref[t0 + lo + j - 2:t0 + lo + j - 2 + n, :]
            if c == 0:
                r = lax.broadcasted_iota(jnp.int32, x.shape, 0) + (lo + j - 2)
                x = jnp.where(r >= N_PAD, x, 0.0)
            acc = acc + cw[j:j + 1] * x
        put(xc_ref, slice(t0 + lo, t0 + hi), acc)
        if c == n_conv - 1:
            base = lp - 8
            tail = xb_ref[base:lp, :]
            prev = xb_ref[base - 8:base, :]
            r = lax.broadcasted_iota(jnp.int32, tail.shape, 0)
            x_m2 = jnp.where(r >= 2, pltpu.roll(tail, 2, 0), pltpu.roll(prev, 2, 0))
            x_m1 = jnp.where(r >= 1, pltpu.roll(tail, 1, 0), pltpu.roll(prev, 1, 0))
            x_p1 = jnp.where(r < 7, pltpu.roll(tail, 7, 0), 0.0)
            put(xc_ref, slice(base, lp), cb + cw[0:1] * x_m2 + cw[1:2] * x_m1 + cw[2:3] * tail + cw[3:4] * x_p1)
    put(xc_ref, slice(0, N_PAD), jnp.zeros((N_PAD, width), F32))

    sc = 8 * seg
    n_chunks = lp // sc

    def strided(t0, j):
        return pl.ds(t0 + j, 8, stride=seg)

    def load_perm(ref, t0):
        halves = [jnp.concatenate([ref[hh, strided(t0, j), :] for j in range(seg)], axis=0) for hh in range(2)]
        return jnp.concatenate(halves, axis=1)

    def gates(xc, d, rate):
        xcb = xc.astype(BF16)
        r = jax.nn.sigmoid(_dot(xcb, wg_ref[d, 0]) + bg_ref[2 * d:2 * d + 1, :])
        i = jax.nn.sigmoid(_dot(xcb, wg_ref[d, 1]) + bg_ref[2 * d + 1:2 * d + 2, :])
        a = jnp.exp2(r * rate)
        x = 1.0 - a * a
        u = jnp.where(x > 0.0, x * lax.rsqrt(x), 0.0) * (i * xc)
        return a, u

    def scan_chunk(a, u, carry, reverse):
        hs, ps = [None] * seg, [None] * seg
        h = p = None
        for j in (reversed(range(seg)) if reverse else range(seg)):
            aj, uj = a[8 * j:8 * j + 8], u[8 * j:8 * j + 8]
            h = uj if h is None else aj * h + uj
            p = aj if p is None else aj * p
            hs[j], ps[j] = h, p
        ends = _group_scan(p, h, carry, reverse)
        row = lax.broadcasted_iota(jnp.int32, ends.shape, 0)
        if reverse:
            entering = jnp.where(row == 7, carry, pltpu.roll(ends, 7, 0))
            carry = ends[0:1]
        else:
            entering = jnp.where(row == 0, carry, pltpu.roll(ends, 1, 0))
            carry = ends[7:8]
        return [hs[j] + ps[j] * entering for j in range(seg)], carry

    def rate(d):
        return _softplus(-lam_ref[d:d + 1, :]) * (-LRU_C * LOG2E)

    rates = (rate(0), rate(1))

    def put_perm(ref, t0, ys):
        for j in range(seg):
            ref[0, strided(t0, j), :] = ys[j][:, :HALF]
            ref[1, strided(t0, j), :] = ys[j][:, HALF:]

    def both(c, carries):
        cf, cbk = carries
        tf = pl.multiple_of(c * sc, 8)
        tb = pl.multiple_of((n_chunks - 1 - c) * sc, 8)
        af, uf = gates(load_perm(xc_ref, tf), 0, rates[0])
        ab, ub = gates(load_perm(xc_ref, tb), 1, rates[1])
        yf, cf = scan_chunk(af, uf, cf, False)
        yb, cbk = scan_chunk(ab, ub, cbk, True)
        put_perm(y_ref, tf, yf)
        put_perm(yb_ref, tb, yb)
        return cf, cbk

    zero = jnp.zeros((1, width), F32)
    lax.fori_loop(0, n_chunks, both, (zero, zero))

    def finish(c, carry):
        rows = pl.ds(pl.multiple_of(c * ch, 128), ch)
        y = jnp.concatenate([y_ref[0, rows, :] + yb_ref[0, rows, :],
                             y_ref[1, rows, :] + yb_ref[1, rows, :]], axis=1)
        o_ref[rows, :] = (y * gate_ref[rows, :].astype(F32)).astype(BF16)
        return carry

    lax.fori_loop(0, lp // ch, finish, 0)


def _lru_scan(xb, gate, cw, cb, wg, bg, lam):
    b, lp, _ = xb.shape
    w = LRU_BLOCK_W
    seg = _odd_part(lp // 8)
    blk = pl.BlockSpec((None, lp, w), lambda i, j: (i, 0, j))
    col = lambda r: pl.BlockSpec((r, w), lambda i, j: (0, j))
    return pl.pallas_call(
        functools.partial(_lru_scan_kernel, lp=lp, seg=seg),
        grid=(b, LRU_BLOCKS),
        in_specs=[blk, blk, col(4), col(1),
                  pl.BlockSpec((2, 2, None, w, w), lambda i, j: (0, 0, j, 0, 0)),
                  col(4), col(2)],
        out_specs=blk,
        out_shape=jax.ShapeDtypeStruct((b, lp, LRU_WIDTH), BF16),
        scratch_shapes=[pltpu.VMEM((2, lp, HALF), F32)] * 3,
        compiler_params=_cparams(2),
        name="lru_scan",
    )(xb, gate, cw, cb, wg, bg, lam)


def _post_kernel(o_ref, h_ref, wo_ref, g_ref, win_ref, wout_ref, out_ref):
    g = g_ref[...]
    x = h_ref[...] + _rms(_dot(o_ref[...], wo_ref[...]), g[1:2])
    xn = _rms(x, g[2:3]).astype(BF16)
    gg = _dot(xn, win_ref[:, :FFN_HIDDEN])
    uu = _dot(xn, win_ref[:, FFN_HIDDEN:])
    a = (gg * jax.nn.sigmoid(gg) * uu).astype(BF16)
    out_ref[...] = x + _rms(_dot(a, wout_ref[...]), g[3:4])


def _post(o, h, w_o, j, g, w_in, w_out, i):
    rows = h.shape[0]
    tm = POST_TILE if rows % POST_TILE == 0 else ROW_TILE
    row = lambda r: (r, 0)
    once = dict(pipeline_mode=pl.Buffered(1))
    return pl.pallas_call(
        _post_kernel,
        grid=(rows // tm,),
        in_specs=[pl.BlockSpec((tm, o.shape[1]), row), pl.BlockSpec((tm, D_MODEL), row),
                  _layer(w_o, j, **once), _full(g.shape), _layer(w_in, i, **once), _layer(w_out, i, **once)],
        out_specs=pl.BlockSpec((tm, D_MODEL), row),
        out_shape=jax.ShapeDtypeStruct((rows, D_MODEL), F32),
        compiler_params=_cparams(1),
        name="post_ffn",
    )(o, h, w_o, g, w_in, w_out)


def _rope_rows(lp):
    pos = jnp.arange(lp, dtype=F32) - float(N_PAD)
    inv_freq = ROPE_THETA ** (-jnp.arange(0, MLA_ROPE, 2, dtype=F32) / MLA_ROPE)
    ang = pos[:, None] * inv_freq[None, :]
    c, s = jnp.cos(ang), jnp.sin(ang)
    z32 = jnp.zeros_like(c)
    z64 = jnp.zeros((lp, 64), F32)
    return (jnp.concatenate([c, c, z64], axis=1),
            jnp.concatenate([-s, z32, z64], axis=1),
            jnp.concatenate([z32, s, z64], axis=1))


def kernel(x, meta_tokens, norm_g, mla_w_in, mla_q_norm, mla_kv_norm, mla_w_uq, mla_w_ukv, mla_w_o,
           diff_w_in, diff_lambda, diff_subln, diff_w_o, lru_w_in, lru_conv_w, lru_conv_b,
           lru_w_gates, lru_b_gates, lru_lambda, lru_w_o, ffn_w_in, ffn_w_out):
    b, seq, d = x.shape
    depth = norm_g.shape[0]
    lp = seq + ROW_OFF
    assert d == D_MODEL and seq % 128 == 0 and lp % LRU_CHUNK == 0 and lp % ROW_TILE == 0

    meta = jnp.broadcast_to(meta_tokens[None].astype(x.dtype), (b, N_META, d))
    h = jnp.concatenate([jnp.zeros((b, N_PAD, d), x.dtype), meta, x], axis=1).reshape(b * lp, d)
    cos, slo, shi = _rope_rows(lp)
    slopes = 2.0 ** (-8.0 * jnp.arange(1, DIFF_HEADS + 1, dtype=F32) / DIFF_HEADS)

    n_mla = mla_w_in.shape[0]
    m_in = jnp.pad(mla_w_in, ((0, 0), (0, 0), (0, 64))).astype(BF16)
    uq = mla_w_uq.reshape(n_mla, MLA_Q_RANK, MLA_HEADS, MLA_NOPE + MLA_ROPE)
    uq = jnp.pad(uq, ((0, 0), (0, 0), (0, 0), (0, MLA_QK - MLA_NOPE - MLA_ROPE)))
    m_uq = uq.reshape(n_mla, MLA_Q_RANK, MLA_HEADS * MLA_QK).astype(BF16)
    ukv = mla_w_ukv.reshape(n_mla, MLA_KV_RANK, MLA_HEADS, MLA_NOPE + MLA_V)
    m_uk = ukv[..., :MLA_NOPE].reshape(n_mla, MLA_KV_RANK, MLA_HEADS * MLA_NOPE).astype(BF16)
    m_uv = ukv[..., MLA_NOPE:].reshape(n_mla, MLA_KV_RANK, MLA_HEADS * MLA_V).astype(BF16)
    m_qg, m_kvg = mla_q_norm[:, None, :], mla_kv_norm[:, None, :]
    w_o_stacks = (mla_w_o.astype(BF16), diff_w_o.astype(BF16), lru_w_o.astype(BF16))
    d_in, l_in = diff_w_in.astype(BF16), lru_w_in.astype(BF16)
    f_in, f_out = ffn_w_in.astype(BF16), ffn_w_out.astype(BF16)

    for i in range(depth):
        kind, j = i % 3, i // 3
        g = norm_g[i]
        if kind == 0:
            q, k, v = _mla_prep(h, g[0:1], j, m_in, m_qg, m_kvg, m_uq, m_uk, m_uv, cos, slo, shi, lp)
            o = _mla_attn(q.reshape(b, lp, -1), k.reshape(b, lp, -1), v.reshape(b, lp, -1), seq)
        elif kind == 1:
            lambda_init = 0.8 - 0.6 * math.exp(-0.3 * i)
            q, k, v = _diff_prep(h, g[0:1], d_in, j)
            o = _diff_attn(slopes, q.reshape(b, lp, -1), k.reshape(b, lp, -1), v.reshape(b, lp, -1),
                           diff_lambda[j], diff_subln[j][None], seq, lambda_init)
        else:
            xb, gate = _lru_prep(h, g[0:1], l_in, j)
            o = _lru_scan(xb.reshape(b, lp, -1), gate.reshape(b, lp, -1), lru_conv_w[j],
                          lru_conv_b[j][None], lru_w_gates[j].astype(BF16),
                          lru_b_gates[j].reshape(4, LRU_WIDTH), lru_lambda[j])
        h = _post(o.reshape(b * lp, -1), h, w_o_stacks[kind], j, g, f_in, f_out, i)
    return h.reshape(b, lp, d)[:, ROW_OFF:]
```

```python
import functools
import math

import jax
import jax.numpy as jnp
import numpy as np
from jax import lax
from jax.experimental import pallas as pl
from jax.experimental.pallas import tpu as pltpu

F32 = jnp.float32
BF16 = jnp.bfloat16

D_MODEL = 1024
N_META = 16
ROW_OFF = 128
N_PAD = ROW_OFF - N_META
NORM_EPS = 1e-6
LOG2E = math.log2(math.e)
NEG_BIG = -1e30

MLA_HEADS = 16
MLA_Q_RANK = 384
MLA_KV_RANK = 256
MLA_NOPE = 128
MLA_ROPE = 64
MLA_V = 128
MLA_QK = 256
ROPE_THETA = 10000.0

DIFF_HEADS = 8
DIFF_HEAD_DIM = 64

LRU_WIDTH = 1536
LRU_BLOCKS = 6
LRU_BLOCK_W = 256
LRU_C = 8.0
LRU_CHUNK = 384
HALF = 128
LRU_UNROLL = 4

FFN_HIDDEN = 2816

ROW_TILE = 384
POST_TILE = 512
VMEM_LIMIT = 56 * 1024 * 1024


def _cparams(n_axes):
    return pltpu.CompilerParams(dimension_semantics=("arbitrary",) * n_axes,
                                vmem_limit_bytes=VMEM_LIMIT)


def _rms(x, g):
    return x * lax.rsqrt(jnp.mean(x * x, axis=-1, keepdims=True) + NORM_EPS) * g


def _dot(a, b):
    return jnp.dot(a, b, preferred_element_type=F32)


def _dot_nt(a, b):
    return lax.dot_general(a, b, (((1,), (1,)), ((), ())), preferred_element_type=F32)


def _full(shape):
    nd = len(shape)
    return pl.BlockSpec(shape, lambda *_: (0,) * nd)


def _layer(a, i, **kw):
    nd = a.ndim - 1
    return pl.BlockSpec((None,) + a.shape[1:], lambda *_: (i,) + (0,) * nd, **kw)


def _rope128(v, cos, sin_lo, sin_hi):
    return v * cos + pltpu.roll(v, 96, 1) * sin_lo + pltpu.roll(v, 32, 1) * sin_hi


def _mla_prep_kernel(h_ref, g_ref, win_ref, qg_ref, kvg_ref, wuq_ref, wuk_ref, wuv_ref,
                     cos_ref, slo_ref, shi_ref, q_ref, k_ref, v_ref, *, q_scale):
    u = _rms(h_ref[...], g_ref[...]).astype(BF16)
    t = _dot(u, win_ref[...])
    cq = _rms(t[:, :MLA_Q_RANK], qg_ref[...]).astype(BF16)
    ckv = _rms(t[:, MLA_Q_RANK:MLA_Q_RANK + MLA_KV_RANK], kvg_ref[...]).astype(BF16)
    cos, slo, shi = cos_ref[...], slo_ref[...], shi_ref[...]
    kr = _rope128(t[:, MLA_Q_RANK + MLA_KV_RANK:], cos, slo, shi).astype(BF16)
    for hp in range(MLA_HEADS // 2):
        kn = _dot(ckv, wuk_ref[:, hp * 256:(hp + 1) * 256])
        v_ref[:, hp * 256:(hp + 1) * 256] = _dot(ckv, wuv_ref[:, hp * 256:(hp + 1) * 256]).astype(BF16)
        for j in range(2):
            h = 2 * hp + j
            q = _dot(cq, wuq_ref[:, h * MLA_QK:(h + 1) * MLA_QK])
            q_ref[:, h * MLA_QK:h * MLA_QK + 128] = (q[:, :128] * q_scale).astype(BF16)
            q_ref[:, h * MLA_QK + 128:(h + 1) * MLA_QK] = (
                _rope128(q[:, 128:], cos, slo, shi) * q_scale).astype(BF16)
            k_ref[:, h * MLA_QK:h * MLA_QK + 128] = kn[:, j * 128:(j + 1) * 128].astype(BF16)
            k_ref[:, h * MLA_QK + 128:(h + 1) * MLA_QK] = kr


def _mla_prep(h, g, j, w_in, qg, kvg, wuq, wuk, wuv, cos, slo, shi, lp):
    rows = h.shape[0]
    tm = ROW_TILE
    tiles_per_seq = lp // tm
    row = lambda i: (i, 0)
    pos = lambda i: (i % tiles_per_seq, 0)
    q_scale = (MLA_NOPE + MLA_ROPE) ** -0.5 * LOG2E
    return pl.pallas_call(
        functools.partial(_mla_prep_kernel, q_scale=q_scale),
        grid=(rows // tm,),
        in_specs=[pl.BlockSpec((tm, D_MODEL), row), _full(g.shape), _layer(w_in, j),
                  _layer(qg, j), _layer(kvg, j), _layer(wuq, j), _layer(wuk, j),
                  _layer(wuv, j), pl.BlockSpec((tm, 128), pos), pl.BlockSpec((tm, 128), pos),
                  pl.BlockSpec((tm, 128), pos)],
        out_specs=[pl.BlockSpec((tm, MLA_HEADS * MLA_QK), row),
                   pl.BlockSpec((tm, MLA_HEADS * MLA_QK), row),
                   pl.BlockSpec((tm, MLA_HEADS * MLA_V), row)],
        out_shape=[jax.ShapeDtypeStruct((rows, MLA_HEADS * MLA_QK), BF16),
                   jax.ShapeDtypeStruct((rows, MLA_HEADS * MLA_QK), BF16),
                   jax.ShapeDtypeStruct((rows, MLA_HEADS * MLA_V), BF16)],
        compiler_params=_cparams(1),
        name="mla_prep",
    )(h, g, w_in, qg, kvg, wuq, wuk, wuv, cos, slo, shi)


def _v_ones(v_ref, k0, n):
    return jnp.concatenate([v_ref[pl.ds(k0, n), :], jnp.ones((n, v_ref.shape[1]), BF16)], axis=1)


def _query_tiles(q_tile, o_ref, seq, tq):
    o_ref[0:N_PAD, :] = jnp.zeros((N_PAD, o_ref.shape[1]), o_ref.dtype)
    q_tile(N_PAD, N_META + tq)

    def body(i, carry):
        q_tile(pl.multiple_of(ROW_OFF + i * tq, 128), tq)
        return carry

    lax.fori_loop(1, seq // tq, body, 0)


def _pick_tile(seq, largest):
    t = largest
    while seq % t:
        t //= 2
    return t


def _mla_attn_kernel(q_ref, k_ref, v_ref, o_ref, *, seq, tq, tk):
    dv = v_ref.shape[1]

    def q_tile(r0, rows):
        q = q_ref[pl.ds(r0, rows), :]
        s = _dot_nt(q, k_ref[0:ROW_OFF, :])
        col = lax.broadcasted_iota(jnp.int32, (rows, ROW_OFF), 1)
        s = jnp.where(col >= N_PAD, s, NEG_BIG)
        m = jnp.max(s, axis=1, keepdims=True)
        acc = _dot(jnp.exp2(s - m).astype(BF16), _v_ones(v_ref, 0, ROW_OFF))
        for c in range(seq // tk):
            k0 = ROW_OFF + c * tk
            s = _dot_nt(q, k_ref[k0:k0 + tk, :])
            m_new = jnp.maximum(m, jnp.max(s, axis=1, keepdims=True))
            p = jnp.exp2(s - m_new)
            acc = jnp.exp2(m - m_new) * acc + _dot(p.astype(BF16), _v_ones(v_ref, k0, tk))
            m = m_new
        o_ref[pl.ds(r0, rows), :] = (acc[:, :dv] / acc[:, dv:]).astype(BF16)

    _query_tiles(q_tile, o_ref, seq, tq)


def _mla_attn(q, k, v, seq):
    b, lp, _ = q.shape
    tq = _pick_tile(seq, 4096)
    tk = _pick_tile(seq, 256)
    return pl.pallas_call(
        functools.partial(_mla_attn_kernel, seq=seq, tq=tq, tk=tk),
        grid=(b, MLA_HEADS),
        in_specs=[pl.BlockSpec((None, lp, MLA_QK), lambda i, j: (i, 0, j)),
                  pl.BlockSpec((None, lp, MLA_QK), lambda i, j: (i, 0, j)),
                  pl.BlockSpec((None, lp, MLA_V), lambda i, j: (i, 0, j))],
        out_specs=pl.BlockSpec((None, lp, MLA_V), lambda i, j: (i, 0, j)),
        out_shape=jax.ShapeDtypeStruct((b, lp, MLA_HEADS * MLA_V), BF16),
        compiler_params=_cparams(2),
        name="mla_attn",
    )(q, k, v)


def _diff_prep_kernel(h_ref, g_ref, w_ref, q_ref, k_ref, v_ref, *, q_scale):
    u = _rms(h_ref[...], g_ref[...]).astype(BF16)
    n = q_ref.shape[1]
    for c in range(n // 256):
        sl = slice(c * 256, (c + 1) * 256)
        q_ref[:, sl] = (_dot(u, w_ref[:, c * 256:(c + 1) * 256]) * q_scale).astype(BF16)
        k_ref[:, sl] = _dot(u, w_ref[:, n + c * 256:n + (c + 1) * 256]).astype(BF16)
        v_ref[:, sl] = _dot(u, w_ref[:, 2 * n + c * 256:2 * n + (c + 1) * 256]).astype(BF16)


def _diff_prep(h, g, w, j):
    rows = h.shape[0]
    tm = ROW_TILE
    n = w.shape[2] // 3
    row = lambda i: (i, 0)
    return pl.pallas_call(
        functools.partial(_diff_prep_kernel, q_scale=DIFF_HEAD_DIM ** -0.5 * LOG2E),
        grid=(rows // tm,),
        in_specs=[pl.BlockSpec((tm, D_MODEL), row), _full(g.shape), _layer(w, j)],
        out_specs=[pl.BlockSpec((tm, n), row)] * 3,
        out_shape=[jax.ShapeDtypeStruct((rows, n), BF16)] * 3,
        compiler_params=_cparams(1),
        name="diff_prep",
    )(h, g, w)


BIAS_SPLIT = 32


def _split3(x):
    c1 = x.astype(BF16).astype(F32)
    c2 = (x - c1).astype(BF16).astype(F32)
    c3 = (x - c1 - c2).astype(BF16).astype(F32)
    return c1, c2, c3


def _diff_attn_kernel(slope_ref, q_ref, k_ref, v_ref, lam_ref, g_ref, o_ref, *,
                      seq, tq, tk, lambda_init):
    slope2 = slope_ref[pl.program_id(1)] * LOG2E
    lp4 = lam_ref[...]
    lam = (jnp.exp(jnp.sum(lp4[0:1] * lp4[1:2], axis=1, keepdims=True))
           - jnp.exp(jnp.sum(lp4[2:3] * lp4[3:4], axis=1, keepdims=True)) + lambda_init)
    g = g_ref[...]
    dv = v_ref.shape[1]
    n_chunks = seq // tk
    per_tile = tq // tk

    def lanes3(lane, x1, x2, x3):
        return jnp.where(lane % 3 == 0, x1, jnp.where(lane % 3 == 1, x2, x3))

    c1, c2, c3 = _split3(jnp.full((1, 128), slope2, F32))
    lane_k = lax.broadcasted_iota(jnp.int32, (tk, 128), 1)
    jj = lax.broadcasted_iota(jnp.int32, (tk, 128), 0).astype(F32)
    piece = lanes3(lane_k, c1, c2, c3)
    k_const = jnp.where(lane_k < 3, jj,
                        jnp.where(lane_k < 6, float(BIAS_SPLIT) * piece,
                                  jnp.where(lane_k < 9, piece, 0.0))).astype(BF16)

    def k_aug(k0, n, const):
        return jnp.concatenate([k_ref[pl.ds(k0, n), :], const], axis=1)

    def q_tile(r0, rows, t):
        first = isinstance(t, int)
        q = q_ref[pl.ds(r0, rows), :]
        lane = lax.broadcasted_iota(jnp.int32, q.shape, 1)
        zero = jnp.zeros_like(q)
        qs = (jnp.where(lane < DIFF_HEAD_DIM, q, zero), jnp.where(lane >= DIFF_HEAD_DIM, q, zero))
        ii = lax.broadcasted_iota(jnp.int32, (rows, 128), 0)
        if first:
            ii = ii - N_META
        lane_q = lax.broadcasted_iota(jnp.int32, (rows, 128), 1)
        in_a = (lane_q >= 3) & (lane_q < 6)
        base = jnp.where(lane_q < 3, lanes3(lane_q, c1, c2, c3),
                         jnp.where(in_a, -(ii >> 5).astype(F32),
                                   jnp.where(lane_q < 9, -(ii & (BIAS_SPLIT - 1)).astype(F32), 0.0))
                         ).astype(BF16)
        unit = jnp.where(in_a, -1.0, 0.0).astype(BF16)

        def q_aug(sgn, n_hi):
            tt = base + unit * (n_hi if isinstance(n_hi, float) else n_hi.astype(BF16))
            if not isinstance(sgn, float):
                tt = tt * sgn.astype(BF16)
            elif sgn != 1.0:
                tt = tt * sgn
            tt = tt.astype(BF16)
            return [jnp.concatenate([qs[c], tt], axis=1) for c in range(2)]

        def above_diag(width, off):
            d = (lax.broadcasted_iota(jnp.int32, (rows, width), 1)
                 - lax.broadcasted_iota(jnp.int32, (rows, width), 0))
            if first:
                d = d + N_META
            return jnp.maximum((d - off).astype(F32), 0.0) * (2.0 * slope2)

        base_row = ROW_OFF + t * tq
        n_meta = float(base_row // BIAS_SPLIT) if first else (base_row // BIAS_SPLIT).astype(F32)
        qa = q_aug(1.0, n_meta)
        kk = k_aug(0, ROW_OFF, k_const[0:ROW_OFF])
        col = lax.broadcasted_iota(jnp.int32, (rows, ROW_OFF), 1)
        vv = _v_ones(v_ref, 0, ROW_OFF)
        fix = above_diag(ROW_OFF, ROW_OFF) if first else None
        m, acc = [], []
        for c in range(2):
            s = _dot_nt(qa[c], kk)
            if first:
                s = s - fix
            s = jnp.where(col >= N_PAD, s, NEG_BIG)
            mc = jnp.max(s, axis=1, keepdims=True)
            m.append(mc)
            acc.append(_dot(jnp.exp2(s - mc).astype(BF16), vv))

        for rel in range(n_chunks):
            diag = rel < per_tile
            if first:
                sgn = 1.0 if diag else -1.0
                n_hi = float(-(rel * tk) // BIAS_SPLIT)
                k0 = ROW_OFF + rel * tk
            else:
                raw = t * per_tile + rel
                wrapped = raw >= n_chunks
                c_idx = jnp.where(wrapped, raw - n_chunks, raw)
                sgn = 1.0 if diag else jnp.where(wrapped, 1.0, -1.0)
                n_hi = ((t * per_tile - c_idx) * (tk // BIAS_SPLIT)).astype(F32)
                k0 = pl.multiple_of(ROW_OFF + c_idx * tk, 128)
            qa = q_aug(sgn, n_hi)
            kk = k_aug(k0, tk, k_const)
            vv = _v_ones(v_ref, k0, tk)
            fix = above_diag(tk, -rel * tk) if diag else None
            for c in range(2):
                s = _dot_nt(qa[c], kk)
                if diag:
                    s = s - fix
                m_new = jnp.maximum(m[c], jnp.max(s, axis=1, keepdims=True))
                p = jnp.exp2(s - m_new)
                acc[c] = jnp.exp2(m[c] - m_new) * acc[c] + _dot(p.astype(BF16), vv)
                m[c] = m_new
        o = acc[0][:, :dv] / acc[0][:, dv:] - lam * (acc[1][:, :dv] / acc[1][:, dv:])
        o = _rms(o, g) * (1.0 - lambda_init)
        o_ref[pl.ds(r0, rows), :] = o.astype(BF16)

    o_ref[0:N_PAD, :] = jnp.zeros((N_PAD, o_ref.shape[1]), o_ref.dtype)
    q_tile(N_PAD, N_META + tq, 0)

    def body(i, carry):
        q_tile(pl.multiple_of(ROW_OFF + i * tq, 128), tq, i)
        return carry

    lax.fori_loop(1, seq // tq, body, 0)


def _diff_attn(slopes, q, k, v, lam_p, g, seq, lambda_init):
    b, lp, _ = q.shape
    tq = _pick_tile(seq, 512)
    tk = _pick_tile(seq, 256)
    w = 2 * DIFF_HEAD_DIM
    blk = pl.BlockSpec((None, lp, w), lambda i, j: (i, 0, j))
    return pl.pallas_call(
        functools.partial(_diff_attn_kernel, seq=seq, tq=tq, tk=tk, lambda_init=lambda_init),
        grid=(b, DIFF_HEADS),
        in_specs=[pl.BlockSpec(memory_space=pltpu.SMEM), blk, blk, blk,
                  _full(lam_p.shape), _full(g.shape)],
        out_specs=blk,
        out_shape=jax.ShapeDtypeStruct((b, lp, DIFF_HEADS * w), BF16),
        compiler_params=_cparams(2),
        name="diff_attn",
    )(slopes, q, k, v, lam_p, g)


def _gelu_tanh(x):
    c = math.sqrt(2.0 / math.pi)
    return x * (0.5 * (1.0 + jnp.tanh(c * (x + 0.044715 * (x * x * x)))))


def _lru_prep_kernel(h_ref, g_ref, w_ref, xb_ref, gate_ref):
    u = _rms(h_ref[...], g_ref[...]).astype(BF16)
    for c in range(LRU_WIDTH // 256):
        sl = slice(c * 256, (c + 1) * 256)
        xb_ref[:, sl] = _dot(u, w_ref[:, c * 256:(c + 1) * 256])
        gate = _dot(u, w_ref[:, LRU_WIDTH + c * 256:LRU_WIDTH + (c + 1) * 256])
        gate_ref[:, sl] = _gelu_tanh(gate).astype(BF16)


def _lru_prep(h, g, w, j):
    rows = h.shape[0]
    tm = ROW_TILE
    row = lambda i: (i, 0)
    return pl.pallas_call(
        _lru_prep_kernel,
        grid=(rows // tm,),
        in_specs=[pl.BlockSpec((tm, D_MODEL), row), _full(g.shape), _layer(w, j)],
        out_specs=[pl.BlockSpec((tm, LRU_WIDTH), row)] * 2,
        out_shape=[jax.ShapeDtypeStruct((rows, LRU_WIDTH), F32),
                   jax.ShapeDtypeStruct((rows, LRU_WIDTH), BF16)],
        compiler_params=_cparams(1),
        name="lru_prep",
    )(h, g, w)


def _softplus(x):
    return jnp.maximum(x, 0.0) + jnp.log(1.0 + jnp.exp(-jnp.abs(x)))


def _group_scan(a, u, carry, reverse):
    row = lax.broadcasted_iota(jnp.int32, a.shape, 0)
    for k in (1, 2, 4):
        shift = (8 - k) if reverse else k
        keep = (row < 8 - k) if reverse else (row >= k)
        a_s = pltpu.roll(a, shift, 0)
        u_s = pltpu.roll(u, shift, 0)
        u = jnp.where(keep, a * u_s + u, u)
        a = jnp.where(keep, a * a_s, a)
    return a * carry + u


def _odd_part(n):
    while n % 2 == 0:
        n //= 2
    return n


def _lru_scan_kernel(xa_ref, xh_ref, gate_ref, cw_ref, cb_ref, wg_ref, bg_ref, lam_ref, o_ref,
                     xc_ref, y_ref, yb_ref, *, lp, seg):
    ch = LRU_CHUNK
    width = 2 * HALF
    sc = 8 * seg
    n_chunks = lp // sc
    x_refs = (xa_ref, xh_ref)

    def strided(t0, j):
        return pl.ds(t0 + j, 8, stride=seg)

    row8 = lax.broadcasted_iota(jnp.int32, (8, HALF), 0)
    for c in range(n_chunks):
        t0 = c * sc
        for hh in range(2):
            lanes = slice(hh * HALF, (hh + 1) * HALF)
            cw = cw_ref[:, lanes]
            cb = cb_ref[:, lanes]
            xs = {j: x_refs[hh][strided(t0, j), :] for j in range(-2 if c else 0, seg + (1 if c < n_chunks - 1 else 0))}
            if c == 0:
                for j in (-2, -1):
                    xs[j] = jnp.where(row8 >= 1, pltpu.roll(xs[seg + j], 1, 0), 0.0)
            if c == n_chunks - 1:
                xs[seg] = jnp.where(row8 <= 6, pltpu.roll(xs[0], 7, 0), 0.0)
            for j in range(seg):
                acc = cb + cw[0:1] * xs[j - 2] + cw[1:2] * xs[j - 1] + cw[2:3] * xs[j] + cw[3:4] * xs[j + 1]
                if t0 + j < N_PAD:
                    acc = jnp.where(row8 * seg + (t0 + j) >= N_PAD, acc, 0.0)
                xc_ref[t0 + 8 * j:t0 + 8 * j + 8, lanes] = acc

    def gates(xc, d, rate):
        xcb = xc.astype(BF16)
        r = jax.nn.sigmoid(_dot(xcb, wg_ref[d, 0]) + bg_ref[2 * d:2 * d + 1, :])
        i = jax.nn.sigmoid(_dot(xcb, wg_ref[d, 1]) + bg_ref[2 * d + 1:2 * d + 2, :])
        a = jnp.exp2(r * rate)
        x = 1.0 - a * a
        u = jnp.where(x > 0.0, x * lax.rsqrt(x), 0.0) * (i * xc)
        return a, u

    def scan_chunk(a, u, carry, reverse):
        hs, ps = [None] * seg, [None] * seg
        h = p = None
        for j in (reversed(range(seg)) if reverse else range(seg)):
            aj, uj = a[8 * j:8 * j + 8], u[8 * j:8 * j + 8]
            h = uj if h is None else aj * h + uj
            p = aj if p is None else aj * p
            hs[j], ps[j] = h, p
        ends = _group_scan(p, h, carry, reverse)
        row = lax.broadcasted_iota(jnp.int32, ends.shape, 0)
        if reverse:
            entering = jnp.where(row == 7, carry, pltpu.roll(ends, 7, 0))
            carry = ends[0:1]
        else:
            entering = jnp.where(row == 0, carry, pltpu.roll(ends, 1, 0))
            carry = ends[7:8]
        return [hs[j] + ps[j] * entering for j in range(seg)], carry

    def rate(d):
        return _softplus(-lam_ref[d:d + 1, :]) * (-LRU_C * LOG2E)

    rates = (rate(0), rate(1))

    def put_perm(ref, t0, ys):
        for j in range(seg):
            ref[0, strided(t0, j), :] = ys[j][:, :HALF]
            ref[1, strided(t0, j), :] = ys[j][:, HALF:]

    def both(c, carries):
        cf, cbk = carries
        tf = pl.multiple_of(c * sc, 8)
        tb = pl.multiple_of((n_chunks - 1 - c) * sc, 8)
        af, uf = gates(xc_ref[pl.ds(tf, sc), :], 0, rates[0])
        ab, ub = gates(xc_ref[pl.ds(tb, sc), :], 1, rates[1])
        yf, cf = scan_chunk(af, uf, cf, False)
        yb, cbk = scan_chunk(ab, ub, cbk, True)
        put_perm(y_ref, tf, yf)
        put_perm(yb_ref, tb, yb)
        return cf, cbk

    zero = jnp.zeros((1, width), F32)
    lax.fori_loop(0, n_chunks, both, (zero, zero), unroll=LRU_UNROLL)

    def finish(c, carry):
        rows = pl.ds(pl.multiple_of(c * ch, 128), ch)
        y = jnp.concatenate([y_ref[0, rows, :] + yb_ref[0, rows, :],
                             y_ref[1, rows, :] + yb_ref[1, rows, :]], axis=1)
        o_ref[rows, :] = (y * gate_ref[rows, :].astype(F32)).astype(BF16)
        return carry

    lax.fori_loop(0, lp // ch, finish, 0)


def _lru_scan(xb, gate, cw, cb, wg, bg, lam):
    b, lp, _ = xb.shape
    w = LRU_BLOCK_W
    seg = _odd_part(lp // 8)
    blk = pl.BlockSpec((None, lp, w), lambda i, j: (i, 0, j))
    half = lambda k: pl.BlockSpec((None, lp, HALF), lambda i, j: (i, 0, 2 * j + k))
    col = lambda r: pl.BlockSpec((r, w), lambda i, j: (0, j))
    return pl.pallas_call(
        functools.partial(_lru_scan_kernel, lp=lp, seg=seg),
        grid=(b, LRU_BLOCKS),
        in_specs=[half(0), half(1), blk, col(4), col(1),
                  pl.BlockSpec((2, 2, None, w, w), lambda i, j: (0, 0, j, 0, 0)),
                  col(4), col(2)],
        out_specs=blk,
        out_shape=jax.ShapeDtypeStruct((b, lp, LRU_WIDTH), BF16),
        scratch_shapes=[pltpu.VMEM((lp, w), F32), pltpu.VMEM((2, lp, HALF), F32), pltpu.VMEM((2, lp, HALF), F32)],
        compiler_params=_cparams(2),
        name="lru_scan",
    )(xb, xb, gate, cw, cb, wg, bg, lam)


def _post_kernel(o_ref, h_ref, wo_ref, g_ref, win_ref, wout_ref, out_ref):
    g = g_ref[...]
    x = h_ref[...] + _rms(_dot(o_ref[...], wo_ref[...]), g[1:2])
    xn = _rms(x, g[2:3]).astype(BF16)
    gg = _dot(xn, win_ref[:, :FFN_HIDDEN])
    uu = _dot(xn, win_ref[:, FFN_HIDDEN:])
    a = (gg * jax.nn.sigmoid(gg) * uu).astype(BF16)
    out_ref[...] = x + _rms(_dot(a, wout_ref[...]), g[3:4])


def _post(o, h, w_o, j, g, w_in, w_out, i):
    rows = h.shape[0]
    tm = POST_TILE if rows % POST_TILE == 0 else ROW_TILE
    row = lambda r: (r, 0)
    once = dict(pipeline_mode=pl.Buffered(1))
    return pl.pallas_call(
        _post_kernel,
        grid=(rows // tm,),
        in_specs=[pl.BlockSpec((tm, o.shape[1]), row), pl.BlockSpec((tm, D_MODEL), row),
                  _layer(w_o, j, **once), _full(g.shape), _layer(w_in, i, **once), _layer(w_out, i, **once)],
        out_specs=pl.BlockSpec((tm, D_MODEL), row),
        out_shape=jax.ShapeDtypeStruct((rows, D_MODEL), F32),
        compiler_params=_cparams(1),
        name="post_ffn",
    )(o, h, w_o, g, w_in, w_out)


def _rope_rows(lp):
    pos = jnp.arange(lp, dtype=F32) - float(N_PAD)
    inv_freq = ROPE_THETA ** (-jnp.arange(0, MLA_ROPE, 2, dtype=F32) / MLA_ROPE)
    ang = pos[:, None] * inv_freq[None, :]
    c, s = jnp.cos(ang), jnp.sin(ang)
    z32 = jnp.zeros_like(c)
    z64 = jnp.zeros((lp, 64), F32)
    return (jnp.concatenate([c, c, z64], axis=1),
            jnp.concatenate([-s, z32, z64], axis=1),
            jnp.concatenate([z32, s, z64], axis=1))


def kernel(x, meta_tokens, norm_g, mla_w_in, mla_q_norm, mla_kv_norm, mla_w_uq, mla_w_ukv, mla_w_o,
           diff_w_in, diff_lambda, diff_subln, diff_w_o, lru_w_in, lru_conv_w, lru_conv_b,
           lru_w_gates, lru_b_gates, lru_lambda, lru_w_o, ffn_w_in, ffn_w_out):
    b, seq, d = x.shape
    depth = norm_g.shape[0]
    lp = seq + ROW_OFF
    assert d == D_MODEL and seq % 128 == 0 and lp % LRU_CHUNK == 0 and lp % ROW_TILE == 0

    meta = jnp.broadcast_to(meta_tokens[None].astype(x.dtype), (b, N_META, d))
    h = jnp.concatenate([jnp.zeros((b, N_PAD, d), x.dtype), meta, x], axis=1).reshape(b * lp, d)
    cos, slo, shi = _rope_rows(lp)
    slopes = 2.0 ** (-8.0 * jnp.arange(1, DIFF_HEADS + 1, dtype=F32) / DIFF_HEADS)

    n_mla = mla_w_in.shape[0]
    m_in = jnp.pad(mla_w_in, ((0, 0), (0, 0), (0, 64))).astype(BF16)
    uq = mla_w_uq.reshape(n_mla, MLA_Q_RANK, MLA_HEADS, MLA_NOPE + MLA_ROPE)
    uq = jnp.pad(uq, ((0, 0), (0, 0), (0, 0), (0, MLA_QK - MLA_NOPE - MLA_ROPE)))
    m_uq = uq.reshape(n_mla, MLA_Q_RANK, MLA_HEADS * MLA_QK).astype(BF16)
    ukv = mla_w_ukv.reshape(n_mla, MLA_KV_RANK, MLA_HEADS, MLA_NOPE + MLA_V)
    m_uk = ukv[..., :MLA_NOPE].reshape(n_mla, MLA_KV_RANK, MLA_HEADS * MLA_NOPE).astype(BF16)
    m_uv = ukv[..., MLA_NOPE:].reshape(n_mla, MLA_KV_RANK, MLA_HEADS * MLA_V).astype(BF16)
    m_qg, m_kvg = mla_q_norm[:, None, :], mla_kv_norm[:, None, :]
    w_o_stacks = (mla_w_o.astype(BF16), diff_w_o.astype(BF16), lru_w_o.astype(BF16))
    d_in, l_in = diff_w_in.astype(BF16), lru_w_in.astype(BF16)
    f_in, f_out = ffn_w_in.astype(BF16), ffn_w_out.astype(BF16)

    for i in range(depth):
        kind, j = i % 3, i // 3
        g = norm_g[i]
        if kind == 0:
            q, k, v = _mla_prep(h, g[0:1], j, m_in, m_qg, m_kvg, m_uq, m_uk, m_uv, cos, slo, shi, lp)
            o = _mla_attn(q.reshape(b, lp, -1), k.reshape(b, lp, -1), v.reshape(b, lp, -1), seq)
        elif kind == 1:
            lambda_init = 0.8 - 0.6 * math.exp(-0.3 * i)
            q, k, v = _diff_prep(h, g[0:1], d_in, j)
            o = _diff_attn(slopes, q.reshape(b, lp, -1), k.reshape(b, lp, -1), v.reshape(b, lp, -1),
                           diff_lambda[j], diff_subln[j][None], seq, lambda_init)
        else:
            xb, gate = _lru_prep(h, g[0:1], l_in, j)
            o = _lru_scan(xb.reshape(b, lp, -1), gate.reshape(b, lp, -1), lru_conv_w[j],
                          lru_conv_b[j][None], lru_w_gates[j].astype(BF16),
                          lru_b_gates[j].reshape(4, LRU_WIDTH), lru_lambda[j])
        h = _post(o.reshape(b * lp, -1), h, w_o_stacks[kind], j, g, f_in, f_out, i)
    return h.reshape(b, lp, d)[:, ROW_OFF:]
```

```python
import functools
import math

import jax
import jax.numpy as jnp
from jax import lax
from jax.experimental import pallas as pl
from jax.experimental.pallas import tpu as pltpu

F32 = jnp.float32
BF16 = jnp.bfloat16

D_MODEL = 1024
N_META = 16
ROW_OFF = 128
N_PAD = ROW_OFF - N_META
NORM_EPS = 1e-6
LOG2E = math.log2(math.e)
NEG_BIG = -1e30

MLA_HEADS = 16
MLA_Q_RANK = 384
MLA_KV_RANK = 256
MLA_NOPE = 128
MLA_ROPE = 64
MLA_V = 128
MLA_QK = 256
ROPE_THETA = 10000.0

DIFF_HEADS = 8
DIFF_HEAD_DIM = 64

LRU_WIDTH = 1536
LRU_BLOCKS = 6
LRU_BLOCK_W = 256
LRU_C = 8.0
LRU_CHUNK = 384
HALF = 128
LRU_UNROLL = 4

FFN_HIDDEN = 2816

ROW_TILE = 384
POST_TILE = 512
VMEM_LIMIT = 56 * 1024 * 1024


def _cparams(n_axes):
    return pltpu.CompilerParams(dimension_semantics=("arbitrary",) * n_axes,
                                vmem_limit_bytes=VMEM_LIMIT)


def _rms(x, g):
    return x * lax.rsqrt(jnp.mean(x * x, axis=-1, keepdims=True) + NORM_EPS) * g


def _dot(a, b):
    return jnp.dot(a, b, preferred_element_type=F32)


def _dot_nt(a, b):
    return lax.dot_general(a, b, (((1,), (1,)), ((), ())), preferred_element_type=F32)


def _full(shape):
    nd = len(shape)
    return pl.BlockSpec(shape, lambda *_: (0,) * nd)


def _layer(a, i, **kw):
    nd = a.ndim - 1
    return pl.BlockSpec((None,) + a.shape[1:], lambda *_: (i,) + (0,) * nd, **kw)


def _rope128(v, cos, sin):
    return v * cos + pltpu.roll(v, 64, 1) * sin


def _mla_prep_kernel(h_ref, g_ref, win_ref, qg_ref, kvg_ref, wuq_ref, wuk_ref, wuv_ref,
                     cos_ref, sin_ref, q_ref, kn_ref, kr_ref, v_ref, *, q_scale):
    u = _rms(h_ref[...], g_ref[...]).astype(BF16)
    t = _dot(u, win_ref[...])
    cq = _rms(t[:, :MLA_Q_RANK], qg_ref[...]).astype(BF16)
    ckv = _rms(t[:, MLA_Q_RANK:MLA_Q_RANK + MLA_KV_RANK], kvg_ref[...]).astype(BF16)
    cos, sin = cos_ref[...], sin_ref[...]
    kr_ref[...] = _rope128(t[:, MLA_Q_RANK + MLA_KV_RANK:], cos, sin).astype(BF16)
    for hp in range(MLA_HEADS // 2):
        sl = slice(hp * 256, (hp + 1) * 256)
        kn_ref[:, sl] = _dot(ckv, wuk_ref[:, sl]).astype(BF16)
        v_ref[:, sl] = _dot(ckv, wuv_ref[:, sl]).astype(BF16)
        for j in range(2):
            h = 2 * hp + j
            q = _dot(cq, wuq_ref[:, h * MLA_QK:(h + 1) * MLA_QK])
            q_ref[:, h * MLA_QK:h * MLA_QK + 128] = (q[:, :128] * q_scale).astype(BF16)
            q_ref[:, h * MLA_QK + 128:(h + 1) * MLA_QK] = (_rope128(q[:, 128:], cos, sin) * q_scale).astype(BF16)


def _mla_prep(h, g, j, w_in, qg, kvg, wuq, wuk, wuv, cos, sin, lp):
    rows = h.shape[0]
    tm = ROW_TILE
    tiles_per_seq = lp // tm
    row = lambda i: (i, 0)
    pos = lambda i: (i % tiles_per_seq, 0)
    q_scale = (MLA_NOPE + MLA_ROPE) ** -0.5 * LOG2E
    widths = (MLA_HEADS * MLA_QK, MLA_HEADS * MLA_NOPE, 128, MLA_HEADS * MLA_V)
    return pl.pallas_call(
        functools.partial(_mla_prep_kernel, q_scale=q_scale),
        grid=(rows // tm,),
        in_specs=[pl.BlockSpec((tm, D_MODEL), row), _full(g.shape), _layer(w_in, j),
                  _layer(qg, j), _layer(kvg, j), _layer(wuq, j), _layer(wuk, j),
                  _layer(wuv, j), pl.BlockSpec((tm, 128), pos), pl.BlockSpec((tm, 128), pos)],
        out_specs=[pl.BlockSpec((tm, n), row) for n in widths],
        out_shape=[jax.ShapeDtypeStruct((rows, n), BF16) for n in widths],
        compiler_params=_cparams(1),
        name="mla_prep",
    )(h, g, w_in, qg, kvg, wuq, wuk, wuv, cos, sin)


def _v_ones(v_ref, k0, n):
    return jnp.concatenate([v_ref[pl.ds(k0, n), :], jnp.ones((n, v_ref.shape[1]), BF16)], axis=1)


def _query_tiles(q_tile, o_ref, seq, tq):
    o_ref[0:N_PAD, :] = jnp.zeros((N_PAD, o_ref.shape[1]), o_ref.dtype)
    q_tile(N_PAD, N_META + tq)

    def body(i, carry):
        q_tile(pl.multiple_of(ROW_OFF + i * tq, 128), tq)
        return carry

    lax.fori_loop(1, seq // tq, body, 0)


def _pick_tile(seq, largest):
    t = largest
    while seq % t:
        t //= 2
    return t


def _mla_attn_kernel(q_ref, kn_ref, kr_ref, v_ref, o_ref, *, seq, tq, tk):
    dv = v_ref.shape[1]

    def keys(k0, n):
        return jnp.concatenate([kn_ref[k0:k0 + n, :], kr_ref[k0:k0 + n, :]], axis=1)

    def q_tile(r0, rows):
        q = q_ref[pl.ds(r0, rows), :]
        s = _dot_nt(q, keys(0, ROW_OFF))
        col = lax.broadcasted_iota(jnp.int32, (rows, ROW_OFF), 1)
        s = jnp.where(col >= N_PAD, s, NEG_BIG)
        m = jnp.max(s, axis=1, keepdims=True)
        acc = _dot(jnp.exp2(s - m).astype(BF16), _v_ones(v_ref, 0, ROW_OFF))
        for c in range(seq // tk):
            k0 = ROW_OFF + c * tk
            s = _dot_nt(q, keys(k0, tk))
            m_new = jnp.maximum(m, jnp.max(s, axis=1, keepdims=True))
            p = jnp.exp2(s - m_new)
            acc = jnp.exp2(m - m_new) * acc + _dot(p.astype(BF16), _v_ones(v_ref, k0, tk))
            m = m_new
        o_ref[pl.ds(r0, rows), :] = (acc[:, :dv] / acc[:, dv:]).astype(BF16)

    _query_tiles(q_tile, o_ref, seq, tq)


def _mla_attn(q, kn, kr, v, seq):
    b, lp, _ = q.shape
    tq = _pick_tile(seq, 4096)
    tk = _pick_tile(seq, 256)
    head = lambda w: pl.BlockSpec((None, lp, w), lambda i, j: (i, 0, j))
    return pl.pallas_call(
        functools.partial(_mla_attn_kernel, seq=seq, tq=tq, tk=tk),
        grid=(b, MLA_HEADS),
        in_specs=[head(MLA_QK), head(MLA_NOPE), pl.BlockSpec((None, lp, 128), lambda i, j: (i, 0, 0)),
                  head(MLA_V)],
        out_specs=head(MLA_V),
        out_shape=jax.ShapeDtypeStruct((b, lp, MLA_HEADS * MLA_V), BF16),
        compiler_params=_cparams(2),
        name="mla_attn",
    )(q, kn, kr, v)


def _diff_prep_kernel(h_ref, g_ref, w_ref, q_ref, k_ref, v_ref, *, q_scale):
    u = _rms(h_ref[...], g_ref[...]).astype(BF16)
    n = q_ref.shape[1]
    for c in range(n // 256):
        sl = slice(c * 256, (c + 1) * 256)
        q_ref[:, sl] = (_dot(u, w_ref[:, c * 256:(c + 1) * 256]) * q_scale).astype(BF16)
        k_ref[:, sl] = _dot(u, w_ref[:, n + c * 256:n + (c + 1) * 256]).astype(BF16)
        v_ref[:, sl] = _dot(u, w_ref[:, 2 * n + c * 256:2 * n + (c + 1) * 256]).astype(BF16)


def _diff_prep(h, g, w, j):
    rows = h.shape[0]
    tm = ROW_TILE
    n = w.shape[2] // 3
    row = lambda i: (i, 0)
    return pl.pallas_call(
        functools.partial(_diff_prep_kernel, q_scale=DIFF_HEAD_DIM ** -0.5 * LOG2E),
        grid=(rows // tm,),
        in_specs=[pl.BlockSpec((tm, D_MODEL), row), _full(g.shape), _layer(w, j)],
        out_specs=[pl.BlockSpec((tm, n), row)] * 3,
        out_shape=[jax.ShapeDtypeStruct((rows, n), BF16)] * 3,
        compiler_params=_cparams(1),
        name="diff_prep",
    )(h, g, w)


BIAS_SPLIT = 32


def _split3(x):
    c1 = x.astype(BF16).astype(F32)
    c2 = (x - c1).astype(BF16).astype(F32)
    c3 = (x - c1 - c2).astype(BF16).astype(F32)
    return c1, c2, c3


def _diff_attn_kernel(slope_ref, q_ref, k_ref, v_ref, lam_ref, g_ref, o_ref, *,
                      seq, tq, tk, lambda_init):
    slope2 = slope_ref[pl.program_id(1)] * LOG2E
    lp4 = lam_ref[...]
    lam = (jnp.exp(jnp.sum(lp4[0:1] * lp4[1:2], axis=1, keepdims=True))
           - jnp.exp(jnp.sum(lp4[2:3] * lp4[3:4], axis=1, keepdims=True)) + lambda_init)
    g = g_ref[...]
    dv = v_ref.shape[1]
    n_chunks = seq // tk
    per_tile = tq // tk

    def lanes3(lane, x1, x2, x3):
        return jnp.where(lane % 3 == 0, x1, jnp.where(lane % 3 == 1, x2, x3))

    c1, c2, c3 = _split3(jnp.full((1, 128), slope2, F32))
    lane_k = lax.broadcasted_iota(jnp.int32, (tk, 128), 1)
    jj = lax.broadcasted_iota(jnp.int32, (tk, 128), 0).astype(F32)
    piece = lanes3(lane_k, c1, c2, c3)
    k_const = jnp.where(lane_k < 3, jj,
                        jnp.where(lane_k < 6, float(BIAS_SPLIT) * piece,
                                  jnp.where(lane_k < 9, piece, 0.0))).astype(BF16)

    def k_aug(k0, n, const):
        return jnp.concatenate([k_ref[pl.ds(k0, n), :], const], axis=1)

    def q_tile(r0, rows, t):
        first = isinstance(t, int)
        q = q_ref[pl.ds(r0, rows), :]
        lane = lax.broadcasted_iota(jnp.int32, q.shape, 1)
        zero = jnp.zeros_like(q)
        qs = (jnp.where(lane < DIFF_HEAD_DIM, q, zero), jnp.where(lane >= DIFF_HEAD_DIM, q, zero))
        ii = lax.broadcasted_iota(jnp.int32, (rows, 128), 0)
        if first:
            ii = ii - N_META
        lane_q = lax.broadcasted_iota(jnp.int32, (rows, 128), 1)
        in_a = (lane_q >= 3) & (lane_q < 6)
        base = jnp.where(lane_q < 3, lanes3(lane_q, c1, c2, c3),
                         jnp.where(in_a, -(ii >> 5).astype(F32),
                                   jnp.where(lane_q < 9, -(ii & (BIAS_SPLIT - 1)).astype(F32), 0.0))
                         ).astype(BF16)
        unit = jnp.where(in_a, -1.0, 0.0).astype(BF16)

        def q_aug(sgn, n_hi):
            tt = base + unit * (n_hi if isinstance(n_hi, float) else n_hi.astype(BF16))
            if not isinstance(sgn, float):
                tt = tt * sgn.astype(BF16)
            elif sgn != 1.0:
                tt = tt * sgn
            tt = tt.astype(BF16)
            return [jnp.concatenate([qs[c], tt], axis=1) for c in range(2)]

        def above_diag(width, off):
            d = (lax.broadcasted_iota(jnp.int32, (rows, width), 1)
                 - lax.broadcasted_iota(jnp.int32, (rows, width), 0))
            if first:
                d = d + N_META
            return jnp.maximum((d - off).astype(F32), 0.0) * (2.0 * slope2)

        base_row = ROW_OFF + t * tq
        n_meta = float(base_row // BIAS_SPLIT) if first else (base_row // BIAS_SPLIT).astype(F32)
        qa = q_aug(1.0, n_meta)
        kk = k_aug(0, ROW_OFF, k_const[0:ROW_OFF])
        col = lax.broadcasted_iota(jnp.int32, (rows, ROW_OFF), 1)
        vv = _v_ones(v_ref, 0, ROW_OFF)
        fix = above_diag(ROW_OFF, ROW_OFF) if first else None
        m, acc = [], []
        for c in range(2):
            s = _dot_nt(qa[c], kk)
            if first:
                s = s - fix
            s = jnp.where(col >= N_PAD, s, NEG_BIG)
            mc = jnp.max(s, axis=1, keepdims=True)
            m.append(mc)
            acc.append(_dot(jnp.exp2(s - mc).astype(BF16), vv))

        for rel in range(n_chunks):
            diag = rel < per_tile
            if first:
                sgn = 1.0 if diag else -1.0
                n_hi = float(-(rel * tk) // BIAS_SPLIT)
                k0 = ROW_OFF + rel * tk
            else:
                raw = t * per_tile + rel
                wrapped = raw >= n_chunks
                c_idx = jnp.where(wrapped, raw - n_chunks, raw)
                sgn = 1.0 if diag else jnp.where(wrapped, 1.0, -1.0)
                n_hi = ((t * per_tile - c_idx) * (tk // BIAS_SPLIT)).astype(F32)
                k0 = pl.multiple_of(ROW_OFF + c_idx * tk, 128)
            qa = q_aug(sgn, n_hi)
            kk = k_aug(k0, tk, k_const)
            vv = _v_ones(v_ref, k0, tk)
            fix = above_diag(tk, -rel * tk) if diag else None
            for c in range(2):
                s = _dot_nt(qa[c], kk)
                if diag:
                    s = s - fix
                m_new = jnp.maximum(m[c], jnp.max(s, axis=1, keepdims=True))
                p = jnp.exp2(s - m_new)
                acc[c] = jnp.exp2(m[c] - m_new) * acc[c] + _dot(p.astype(BF16), vv)
                m[c] = m_new
        o = acc[0][:, :dv] / acc[0][:, dv:] - lam * (acc[1][:, :dv] / acc[1][:, dv:])
        o = _rms(o, g) * (1.0 - lambda_init)
        o_ref[pl.ds(r0, rows), :] = o.astype(BF16)

    o_ref[0:N_PAD, :] = jnp.zeros((N_PAD, o_ref.shape[1]), o_ref.dtype)
    q_tile(N_PAD, N_META + tq, 0)

    def body(i, carry):
        q_tile(pl.multiple_of(ROW_OFF + i * tq, 128), tq, i)
        return carry

    lax.fori_loop(1, seq // tq, body, 0)


def _diff_attn(slopes, q, k, v, lam_p, g, seq, lambda_init):
    b, lp, _ = q.shape
    tq = _pick_tile(seq, 512)
    tk = _pick_tile(seq, 256)
    assert tk % BIAS_SPLIT == 0 and tk <= 256 and (lp + tq) // BIAS_SPLIT < 256
    w = 2 * DIFF_HEAD_DIM
    blk = pl.BlockSpec((None, lp, w), lambda i, j: (i, 0, j))
    return pl.pallas_call(
        functools.partial(_diff_attn_kernel, seq=seq, tq=tq, tk=tk, lambda_init=lambda_init),
        grid=(b, DIFF_HEADS),
        in_specs=[pl.BlockSpec(memory_space=pltpu.SMEM), blk, blk, blk,
                  _full(lam_p.shape), _full(g.shape)],
        out_specs=blk,
        out_shape=jax.ShapeDtypeStruct((b, lp, DIFF_HEADS * w), BF16),
        compiler_params=_cparams(2),
        name="diff_attn",
    )(slopes, q, k, v, lam_p, g)


def _gelu_tanh(x):
    c = math.sqrt(2.0 / math.pi)
    return x * (0.5 * (1.0 + jnp.tanh(c * (x + 0.044715 * (x * x * x)))))


def _lru_prep_kernel(h_ref, g_ref, w_ref, xb_ref, gate_ref):
    u = _rms(h_ref[...], g_ref[...]).astype(BF16)
    for c in range(LRU_WIDTH // 256):
        sl = slice(c * 256, (c + 1) * 256)
        xb_ref[:, sl] = _dot(u, w_ref[:, c * 256:(c + 1) * 256])
        gate = _dot(u, w_ref[:, LRU_WIDTH + c * 256:LRU_WIDTH + (c + 1) * 256])
        gate_ref[:, sl] = _gelu_tanh(gate).astype(BF16)


def _lru_prep(h, g, w, j):
    rows = h.shape[0]
    tm = ROW_TILE
    row = lambda i: (i, 0)
    return pl.pallas_call(
        _lru_prep_kernel,
        grid=(rows // tm,),
        in_specs=[pl.BlockSpec((tm, D_MODEL), row), _full(g.shape), _layer(w, j)],
        out_specs=[pl.BlockSpec((tm, LRU_WIDTH), row)] * 2,
        out_shape=[jax.ShapeDtypeStruct((rows, LRU_WIDTH), F32),
                   jax.ShapeDtypeStruct((rows, LRU_WIDTH), BF16)],
        compiler_params=_cparams(1),
        name="lru_prep",
    )(h, g, w)


def _softplus(x):
    return jnp.maximum(x, 0.0) + jnp.log(1.0 + jnp.exp(-jnp.abs(x)))


def _group_scan(a, u, carry, reverse):
    row = lax.broadcasted_iota(jnp.int32, a.shape, 0)
    for k in (1, 2, 4):
        shift = (8 - k) if reverse else k
        keep = (row < 8 - k) if reverse else (row >= k)
        a_s = pltpu.roll(a, shift, 0)
        u_s = pltpu.roll(u, shift, 0)
        u = jnp.where(keep, a * u_s + u, u)
        a = jnp.where(keep, a * a_s, a)
    return a * carry + u


def _odd_part(n):
    while n % 2 == 0:
        n //= 2
    return n


def _lru_scan_kernel(xa_ref, xh_ref, gate_ref, cw_ref, cb_ref, wg_ref, bg_ref, lam_ref, o_ref,
                     xc_ref, y_ref, yb_ref, *, lp, seg):
    ch = LRU_CHUNK
    width = 2 * HALF
    sc = 8 * seg
    n_chunks = lp // sc
    x_refs = (xa_ref, xh_ref)

    def strided(t0, j):
        return pl.ds(t0 + j, 8, stride=seg)

    row8 = lax.broadcasted_iota(jnp.int32, (8, HALF), 0)
    for c in range(n_chunks):
        t0 = c * sc
        for hh in range(2):
            lanes = slice(hh * HALF, (hh + 1) * HALF)
            cw = cw_ref[:, lanes]
            cb = cb_ref[:, lanes]
            xs = {j: x_refs[hh][strided(t0, j), :] for j in range(-2 if c else 0, seg + (1 if c < n_chunks - 1 else 0))}
            if c == 0:
                for j in (-2, -1):
                    xs[j] = jnp.where(row8 >= 1, pltpu.roll(xs[seg + j], 1, 0), 0.0)
            if c == n_chunks - 1:
                xs[seg] = jnp.where(row8 <= 6, pltpu.roll(xs[0], 7, 0), 0.0)
            for j in range(seg):
                acc = cb + cw[0:1] * xs[j - 2] + cw[1:2] * xs[j - 1] + cw[2:3] * xs[j] + cw[3:4] * xs[j + 1]
                if t0 + j < N_PAD:
                    acc = jnp.where(row8 * seg + (t0 + j) >= N_PAD, acc, 0.0)
                xc_ref[t0 + 8 * j:t0 + 8 * j + 8, lanes] = acc

    def gates(xc, d, rate):
        xcb = xc.astype(BF16)
        r = jax.nn.sigmoid(_dot(xcb, wg_ref[d, 0]) + bg_ref[2 * d:2 * d + 1, :])
        i = jax.nn.sigmoid(_dot(xcb, wg_ref[d, 1]) + bg_ref[2 * d + 1:2 * d + 2, :])
        a = jnp.exp2(r * rate)
        x = 1.0 - a * a
        u = jnp.where(x > 0.0, x * lax.rsqrt(x), 0.0) * (i * xc)
        return a, u

    def scan_chunk(a, u, carry, reverse):
        hs, ps = [None] * seg, [None] * seg
        h = p = None
        for j in (reversed(range(seg)) if reverse else range(seg)):
            aj, uj = a[8 * j:8 * j + 8], u[8 * j:8 * j + 8]
            h = uj if h is None else aj * h + uj
            p = aj if p is None else aj * p
            hs[j], ps[j] = h, p
        ends = _group_scan(p, h, carry, reverse)
        row = lax.broadcasted_iota(jnp.int32, ends.shape, 0)
        if reverse:
            entering = jnp.where(row == 7, carry, pltpu.roll(ends, 7, 0))
            carry = ends[0:1]
        else:
            entering = jnp.where(row == 0, carry, pltpu.roll(ends, 1, 0))
            carry = ends[7:8]
        return [hs[j] + ps[j] * entering for j in range(seg)], carry

    def rate(d):
        return _softplus(-lam_ref[d:d + 1, :]) * (-LRU_C * LOG2E)

    rates = (rate(0), rate(1))

    def put_perm(ref, t0, ys):
        for j in range(seg):
            ref[0, strided(t0, j), :] = ys[j][:, :HALF]
            ref[1, strided(t0, j), :] = ys[j][:, HALF:]

    def both(c, carries):
        cf, cbk = carries
        tf = pl.multiple_of(c * sc, 8)
        tb = pl.multiple_of((n_chunks - 1 - c) * sc, 8)
        af, uf = gates(xc_ref[pl.ds(tf, sc), :], 0, rates[0])
        ab, ub = gates(xc_ref[pl.ds(tb, sc), :], 1, rates[1])
        yf, cf = scan_chunk(af, uf, cf, False)
        yb, cbk = scan_chunk(ab, ub, cbk, True)
        put_perm(y_ref, tf, yf)
        put_perm(yb_ref, tb, yb)
        return cf, cbk

    zero = jnp.zeros((1, width), F32)
    lax.fori_loop(0, n_chunks, both, (zero, zero), unroll=LRU_UNROLL)

    def finish(c, carry):
        rows = pl.ds(pl.multiple_of(c * ch, 128), ch)
        y = jnp.concatenate([y_ref[0, rows, :] + yb_ref[0, rows, :],
                             y_ref[1, rows, :] + yb_ref[1, rows, :]], axis=1)
        o_ref[rows, :] = (y * gate_ref[rows, :].astype(F32)).astype(BF16)
        return carry

    lax.fori_loop(0, lp // ch, finish, 0)


def _lru_scan(xb, gate, cw, cb, wg, bg, lam):
    b, lp, _ = xb.shape
    w = LRU_BLOCK_W
    seg = _odd_part(lp // 8)
    blk = pl.BlockSpec((None, lp, w), lambda i, j: (i, 0, j))
    half = lambda k: pl.BlockSpec((None, lp, HALF), lambda i, j: (i, 0, 2 * j + k))
    col = lambda r: pl.BlockSpec((r, w), lambda i, j: (0, j))
    return pl.pallas_call(
        functools.partial(_lru_scan_kernel, lp=lp, seg=seg),
        grid=(b, LRU_BLOCKS),
        in_specs=[half(0), half(1), blk, col(4), col(1),
                  pl.BlockSpec((2, 2, None, w, w), lambda i, j: (0, 0, j, 0, 0)),
                  col(4), col(2)],
        out_specs=blk,
        out_shape=jax.ShapeDtypeStruct((b, lp, LRU_WIDTH), BF16),
        scratch_shapes=[pltpu.VMEM((lp, w), F32), pltpu.VMEM((2, lp, HALF), F32), pltpu.VMEM((2, lp, HALF), F32)],
        compiler_params=_cparams(2),
        name="lru_scan",
    )(xb, xb, gate, cw, cb, wg, bg, lam)


def _post_kernel(o_ref, h_ref, wo_ref, g_ref, win_ref, wout_ref, out_ref):
    g = g_ref[...]
    x = h_ref[...] + _rms(_dot(o_ref[...], wo_ref[...]), g[1:2])
    xn = _rms(x, g[2:3]).astype(BF16)
    gg = _dot(xn, win_ref[:, :FFN_HIDDEN])
    uu = _dot(xn, win_ref[:, FFN_HIDDEN:])
    a = (gg * jax.nn.sigmoid(gg) * uu).astype(BF16)
    out_ref[...] = x + _rms(_dot(a, wout_ref[...]), g[3:4])


def _post(o, h, w_o, j, g, w_in, w_out, i, final=None):
    rows = h.shape[0]
    once = dict(pipeline_mode=pl.Buffered(1))
    weights = [_layer(w_o, j, **once), _full(g.shape), _layer(w_in, i, **once), _layer(w_out, i, **once)]
    if final is None:
        tm = POST_TILE if rows % POST_TILE == 0 else ROW_TILE
        grid = (rows // tm,)
        row = lambda r: (r, 0)
        acts = [pl.BlockSpec((tm, o.shape[1]), row), pl.BlockSpec((tm, D_MODEL), row)]
        out_rows, out_spec = rows, pl.BlockSpec((tm, D_MODEL), row)
    else:
        b, lp, seq = final
        tm = _pick_tile(seq, POST_TILE)
        per = seq // tm
        grid = (b, per)
        start = lambda bi, t: (pl.multiple_of(bi * lp + ROW_OFF + t * tm, 128), 0)
        acts = [pl.BlockSpec((pl.Element(tm), pl.Element(o.shape[1])), start),
                pl.BlockSpec((pl.Element(tm), pl.Element(D_MODEL)), start)]
        out_rows, out_spec = b * seq, pl.BlockSpec((tm, D_MODEL), lambda bi, t: (bi * per + t, 0))
    return pl.pallas_call(
        _post_kernel,
        grid=grid,
        in_specs=acts + weights,
        out_specs=out_spec,
        out_shape=jax.ShapeDtypeStruct((out_rows, D_MODEL), F32),
        compiler_params=_cparams(len(grid)),
        name="post_ffn",
    )(o, h, w_o, g, w_in, w_out)


def _rope_rows(lp):
    pos = jnp.arange(lp, dtype=F32) - float(N_PAD)
    inv_freq = ROPE_THETA ** (-jnp.arange(0, MLA_ROPE, 2, dtype=F32) / MLA_ROPE)
    ang = pos[:, None] * inv_freq[None, :]
    c, s = jnp.cos(ang), jnp.sin(ang)
    z64 = jnp.zeros((lp, 64), F32)
    return jnp.concatenate([c, c, z64], axis=1), jnp.concatenate([s, s, z64], axis=1)


def _with_swapped(rope):
    half = rope.shape[-1] // 2
    return jnp.concatenate([rope, -rope[..., half:], rope[..., :half]], axis=-1)


def kernel(x, meta_tokens, norm_g, mla_w_in, mla_q_norm, mla_kv_norm, mla_w_uq, mla_w_ukv, mla_w_o,
           diff_w_in, diff_lambda, diff_subln, diff_w_o, lru_w_in, lru_conv_w, lru_conv_b,
           lru_w_gates, lru_b_gates, lru_lambda, lru_w_o, ffn_w_in, ffn_w_out):
    b, seq, d = x.shape
    depth = norm_g.shape[0]
    lp = seq + ROW_OFF
    assert d == D_MODEL and seq % 128 == 0 and lp % LRU_CHUNK == 0 and lp % ROW_TILE == 0

    meta = jnp.broadcast_to(meta_tokens[None].astype(x.dtype), (b, N_META, d))
    h = jnp.concatenate([jnp.zeros((b, N_PAD, d), x.dtype), meta, x], axis=1).reshape(b * lp, d)
    cos, sin = _rope_rows(lp)
    slopes = 2.0 ** (-8.0 * jnp.arange(1, DIFF_HEADS + 1, dtype=F32) / DIFF_HEADS)

    n_mla = mla_w_in.shape[0]
    lat = MLA_Q_RANK + MLA_KV_RANK
    m_in = jnp.concatenate([mla_w_in[..., :lat], _with_swapped(mla_w_in[..., lat:])], axis=-1).astype(BF16)
    uq = mla_w_uq.reshape(n_mla, MLA_Q_RANK, MLA_HEADS, MLA_NOPE + MLA_ROPE)
    uq = jnp.concatenate([uq[..., :MLA_NOPE], _with_swapped(uq[..., MLA_NOPE:])], axis=-1)
    m_uq = uq.reshape(n_mla, MLA_Q_RANK, MLA_HEADS * MLA_QK).astype(BF16)
    ukv = mla_w_ukv.reshape(n_mla, MLA_KV_RANK, MLA_HEADS, MLA_NOPE + MLA_V)
    m_uk = ukv[..., :MLA_NOPE].reshape(n_mla, MLA_KV_RANK, MLA_HEADS * MLA_NOPE).astype(BF16)
    m_uv = ukv[..., MLA_NOPE:].reshape(n_mla, MLA_KV_RANK, MLA_HEADS * MLA_V).astype(BF16)
    m_qg, m_kvg = mla_q_norm[:, None, :], mla_kv_norm[:, None, :]
    w_o_stacks = (mla_w_o.astype(BF16), diff_w_o.astype(BF16), lru_w_o.astype(BF16))
    d_in, l_in = diff_w_in.astype(BF16), lru_w_in.astype(BF16)
    f_in, f_out = ffn_w_in.astype(BF16), ffn_w_out.astype(BF16)

    for i in range(depth):
        kind, j = i % 3, i // 3
        g = norm_g[i]
        if kind == 0:
            q, kn, kr, v = _mla_prep(h, g[0:1], j, m_in, m_qg, m_kvg, m_uq, m_uk, m_uv, cos, sin, lp)
            o = _mla_attn(q.reshape(b, lp, -1), kn.reshape(b, lp, -1), kr.reshape(b, lp, -1),
                          v.reshape(b, lp, -1), seq)
        elif kind == 1:
            lambda_init = 0.8 - 0.6 * math.exp(-0.3 * i)
            q, k, v = _diff_prep(h, g[0:1], d_in, j)
            o = _diff_attn(slopes, q.reshape(b, lp, -1), k.reshape(b, lp, -1), v.reshape(b, lp, -1),
                           diff_lambda[j], diff_subln[j][None], seq, lambda_init)
        else:
            xb, gate = _lru_prep(h, g[0:1], l_in, j)
            o = _lru_scan(xb.reshape(b, lp, -1), gate.reshape(b, lp, -1), lru_conv_w[j],
                          lru_conv_b[j][None], lru_w_gates[j].astype(BF16),
                          lru_b_gates[j].reshape(4, LRU_WIDTH), lru_lambda[j])
        last = (b, lp, seq) if i == depth - 1 else None
        h = _post(o.reshape(b * lp, -1), h, w_o_stacks[kind], j, g, f_in, f_out, i, final=last)
    return h.reshape(b, seq, d)
```

```python
import functools
import math

import jax
import jax.numpy as jnp
from jax import lax
from jax.experimental import pallas as pl
from jax.experimental.pallas import tpu as pltpu

F32 = jnp.float32
BF16 = jnp.bfloat16

D_MODEL = 1024
N_META = 16
ROW_OFF = 128
N_PAD = ROW_OFF - N_META
NORM_EPS = 1e-6
LOG2E = math.log2(math.e)
NEG_BIG = -1e30

MLA_HEADS = 16
MLA_Q_RANK = 384
MLA_KV_RANK = 256
MLA_NOPE = 128
MLA_ROPE = 64
MLA_V = 128
MLA_QK = 256
ROPE_THETA = 10000.0

DIFF_HEADS = 8
DIFF_HEAD_DIM = 64

LRU_WIDTH = 1536
LRU_BLOCKS = 6
LRU_BLOCK_W = 256
LRU_C = 8.0
LRU_CHUNK = 384
HALF = 128
LRU_UNROLL = 4

FFN_HIDDEN = 2816

ROW_TILE = 384
PREP_TILE = 768
POST_TILE = 512
VMEM_LIMIT = 56 * 1024 * 1024


def _cparams(n_axes):
    return pltpu.CompilerParams(dimension_semantics=("arbitrary",) * n_axes,
                                vmem_limit_bytes=VMEM_LIMIT)


def _rms(x, g):
    return x * lax.rsqrt(jnp.mean(x * x, axis=-1, keepdims=True) + NORM_EPS) * g


def _dot(a, b):
    return jnp.dot(a, b, preferred_element_type=F32)


def _dot_nt(a, b):
    return lax.dot_general(a, b, (((1,), (1,)), ((), ())), preferred_element_type=F32)


def _full(shape):
    nd = len(shape)
    return pl.BlockSpec(shape, lambda *_: (0,) * nd)


def _layer(a, i, **kw):
    nd = a.ndim - 1
    return pl.BlockSpec((None,) + a.shape[1:], lambda *_: (i,) + (0,) * nd, **kw)


def _rope128(v, cos, sin):
    return v * cos + pltpu.roll(v, 64, 1) * sin


def _mla_prep_kernel(h_ref, g_ref, win_ref, qg_ref, kvg_ref, wuq_ref, wuk_ref, wuv_ref,
                     cos_ref, sin_ref, q_ref, kn_ref, kr_ref, v_ref, *, q_scale):
    u = _rms(h_ref[...], g_ref[...]).astype(BF16)
    t = _dot(u, win_ref[...])
    cq = _rms(t[:, :MLA_Q_RANK], qg_ref[...]).astype(BF16)
    ckv = _rms(t[:, MLA_Q_RANK:MLA_Q_RANK + MLA_KV_RANK], kvg_ref[...]).astype(BF16)
    cos, sin = cos_ref[...], sin_ref[...]
    kr_ref[...] = _rope128(t[:, MLA_Q_RANK + MLA_KV_RANK:], cos, sin).astype(BF16)
    for hp in range(MLA_HEADS // 2):
        sl = slice(hp * 256, (hp + 1) * 256)
        kn_ref[:, sl] = _dot(ckv, wuk_ref[:, sl]).astype(BF16)
        v_ref[:, sl] = _dot(ckv, wuv_ref[:, sl]).astype(BF16)
        for j in range(2):
            h = 2 * hp + j
            q = _dot(cq, wuq_ref[:, h * MLA_QK:(h + 1) * MLA_QK])
            q_ref[:, h * MLA_QK:h * MLA_QK + 128] = (q[:, :128] * q_scale).astype(BF16)
            q_ref[:, h * MLA_QK + 128:(h + 1) * MLA_QK] = (_rope128(q[:, 128:], cos, sin) * q_scale).astype(BF16)


def _mla_prep(h, g, j, w_in, qg, kvg, wuq, wuk, wuv, cos, sin, lp):
    rows = h.shape[0]
    tm = lp // 4 if lp % 64 == 0 else ROW_TILE
    tiles_per_seq = lp // tm
    row = lambda i: (i, 0)
    pos = lambda i: (i % tiles_per_seq, 0)
    q_scale = (MLA_NOPE + MLA_ROPE) ** -0.5 * LOG2E
    widths = (MLA_HEADS * MLA_QK, MLA_HEADS * MLA_NOPE, 128, MLA_HEADS * MLA_V)
    return pl.pallas_call(
        functools.partial(_mla_prep_kernel, q_scale=q_scale),
        grid=(rows // tm,),
        in_specs=[pl.BlockSpec((tm, D_MODEL), row), _full(g.shape), _layer(w_in, j),
                  _layer(qg, j), _layer(kvg, j), _layer(wuq, j), _layer(wuk, j),
                  _layer(wuv, j), pl.BlockSpec((tm, 128), pos), pl.BlockSpec((tm, 128), pos)],
        out_specs=[pl.BlockSpec((tm, n), row) for n in widths],
        out_shape=[jax.ShapeDtypeStruct((rows, n), BF16) for n in widths],
        compiler_params=_cparams(1),
        name="mla_prep",
    )(h, g, w_in, qg, kvg, wuq, wuk, wuv, cos, sin)


def _v_ones(v_ref, k0, n):
    return jnp.concatenate([v_ref[pl.ds(k0, n), :], jnp.ones((n, v_ref.shape[1]), BF16)], axis=1)


def _query_tiles(q_tile, o_ref, seq, tq):
    o_ref[0:N_PAD, :] = jnp.zeros((N_PAD, o_ref.shape[1]), o_ref.dtype)
    q_tile(N_PAD, N_META + tq)

    def body(i, carry):
        q_tile(pl.multiple_of(ROW_OFF + i * tq, 128), tq)
        return carry

    lax.fori_loop(1, seq // tq, body, 0)


def _pick_tile(seq, largest):
    t = largest
    while seq % t:
        t //= 2
    return t


def _mla_attn_kernel(q_ref, kn_ref, kr_ref, v_ref, o_ref, *, seq, tq, tk):
    dv = v_ref.shape[1]

    def keys(k0, n):
        return jnp.concatenate([kn_ref[k0:k0 + n, :], kr_ref[k0:k0 + n, :]], axis=1)

    def q_tile(r0, rows):
        q = q_ref[pl.ds(r0, rows), :]
        s = _dot_nt(q, keys(0, ROW_OFF))
        col = lax.broadcasted_iota(jnp.int32, (rows, ROW_OFF), 1)
        s = jnp.where(col >= N_PAD, s, NEG_BIG)
        m = jnp.max(s, axis=1, keepdims=True)
        acc = _dot(jnp.exp2(s - m).astype(BF16), _v_ones(v_ref, 0, ROW_OFF))
        for c in range(seq // tk):
            k0 = ROW_OFF + c * tk
            s = _dot_nt(q, keys(k0, tk))
            m_new = jnp.maximum(m, jnp.max(s, axis=1, keepdims=True))
            p = jnp.exp2(s - m_new)
            acc = jnp.exp2(m - m_new) * acc + _dot(p.astype(BF16), _v_ones(v_ref, k0, tk))
            m = m_new
        o_ref[pl.ds(r0, rows), :] = (acc[:, :dv] / acc[:, dv:]).astype(BF16)

    _query_tiles(q_tile, o_ref, seq, tq)


def _mla_attn(q, kn, kr, v, seq):
    b, lp, _ = q.shape
    tq = _pick_tile(seq, 4096)
    tk = _pick_tile(seq, 256)
    head = lambda w: pl.BlockSpec((None, lp, w), lambda i, j: (i, 0, j))
    return pl.pallas_call(
        functools.partial(_mla_attn_kernel, seq=seq, tq=tq, tk=tk),
        grid=(b, MLA_HEADS),
        in_specs=[head(MLA_QK), head(MLA_NOPE), pl.BlockSpec((None, lp, 128), lambda i, j: (i, 0, 0)),
                  head(MLA_V)],
        out_specs=head(MLA_V),
        out_shape=jax.ShapeDtypeStruct((b, lp, MLA_HEADS * MLA_V), BF16),
        compiler_params=_cparams(2),
        name="mla_attn",
    )(q, kn, kr, v)


def _diff_prep_kernel(h_ref, g_ref, w_ref, q_ref, k_ref, v_ref, *, q_scale):
    u = _rms(h_ref[...], g_ref[...]).astype(BF16)
    n = q_ref.shape[1]
    for c in range(n // 256):
        sl = slice(c * 256, (c + 1) * 256)
        q_ref[:, sl] = (_dot(u, w_ref[:, c * 256:(c + 1) * 256]) * q_scale).astype(BF16)
        k_ref[:, sl] = _dot(u, w_ref[:, n + c * 256:n + (c + 1) * 256]).astype(BF16)
        v_ref[:, sl] = _dot(u, w_ref[:, 2 * n + c * 256:2 * n + (c + 1) * 256]).astype(BF16)


def _diff_prep(h, g, w, j):
    rows = h.shape[0]
    tm = PREP_TILE if rows % PREP_TILE == 0 else ROW_TILE
    n = w.shape[2] // 3
    row = lambda i: (i, 0)
    return pl.pallas_call(
        functools.partial(_diff_prep_kernel, q_scale=DIFF_HEAD_DIM ** -0.5 * LOG2E),
        grid=(rows // tm,),
        in_specs=[pl.BlockSpec((tm, D_MODEL), row), _full(g.shape), _layer(w, j)],
        out_specs=[pl.BlockSpec((tm, n), row)] * 3,
        out_shape=[jax.ShapeDtypeStruct((rows, n), BF16)] * 3,
        compiler_params=_cparams(1),
        name="diff_prep",
    )(h, g, w)


BIAS_SPLIT = 32


def _split3(x):
    c1 = x.astype(BF16).astype(F32)
    c2 = (x - c1).astype(BF16).astype(F32)
    c3 = (x - c1 - c2).astype(BF16).astype(F32)
    return c1, c2, c3


def _diff_attn_kernel(slope_ref, q_ref, k_ref, v_ref, lam_ref, g_ref, o_ref, *,
                      seq, tq, tk, lambda_init):
    slope2 = slope_ref[pl.program_id(1)] * LOG2E
    lp4 = lam_ref[...]
    lam = (jnp.exp(jnp.sum(lp4[0:1] * lp4[1:2], axis=1, keepdims=True))
           - jnp.exp(jnp.sum(lp4[2:3] * lp4[3:4], axis=1, keepdims=True)) + lambda_init)
    g = g_ref[...]
    dv = v_ref.shape[1]
    n_chunks = seq // tk
    per_tile = tq // tk

    def lanes3(lane, x1, x2, x3):
        return jnp.where(lane % 3 == 0, x1, jnp.where(lane % 3 == 1, x2, x3))

    c1, c2, c3 = _split3(jnp.full((1, 128), slope2, F32))
    lane_k = lax.broadcasted_iota(jnp.int32, (tk, 128), 1)
    jj = lax.broadcasted_iota(jnp.int32, (tk, 128), 0).astype(F32)
    piece = lanes3(lane_k, c1, c2, c3)
    k_const = jnp.where(lane_k < 3, jj,
                        jnp.where(lane_k < 6, float(BIAS_SPLIT) * piece,
                                  jnp.where(lane_k < 9, piece, 0.0))).astype(BF16)

    def k_aug(k0, n, const):
        return jnp.concatenate([k_ref[pl.ds(k0, n), :], const], axis=1)

    def q_tile(r0, rows, t):
        first = isinstance(t, int)
        q = q_ref[pl.ds(r0, rows), :]
        lane = lax.broadcasted_iota(jnp.int32, q.shape, 1)
        zero = jnp.zeros_like(q)
        qs = (jnp.where(lane < DIFF_HEAD_DIM, q, zero), jnp.where(lane >= DIFF_HEAD_DIM, q, zero))
        ii = lax.broadcasted_iota(jnp.int32, (rows, 128), 0)
        if first:
            ii = ii - N_META
        lane_q = lax.broadcasted_iota(jnp.int32, (rows, 128), 1)
        in_a = (lane_q >= 3) & (lane_q < 6)
        base = jnp.where(lane_q < 3, lanes3(lane_q, c1, c2, c3),
                         jnp.where(in_a, -(ii >> 5).astype(F32),
                                   jnp.where(lane_q < 9, -(ii & (BIAS_SPLIT - 1)).astype(F32), 0.0))
                         ).astype(BF16)
        unit = jnp.where(in_a, -1.0, 0.0).astype(BF16)

        def q_aug(sgn, n_hi):
            tt = base + unit * (n_hi if isinstance(n_hi, float) else n_hi.astype(BF16))
            if not isinstance(sgn, float):
                tt = tt * sgn.astype(BF16)
            elif sgn != 1.0:
                tt = tt * sgn
            tt = tt.astype(BF16)
            return [jnp.concatenate([qs[c], tt], axis=1) for c in range(2)]

        def above_diag(width, off):
            d = (lax.broadcasted_iota(jnp.int32, (rows, width), 1)
                 - lax.broadcasted_iota(jnp.int32, (rows, width), 0))
            if first:
                d = d + N_META
            return jnp.maximum((d - off).astype(F32), 0.0) * (2.0 * slope2)

        base_row = ROW_OFF + t * tq
        n_meta = float(base_row // BIAS_SPLIT) if first else (base_row // BIAS_SPLIT).astype(F32)
        qa = q_aug(1.0, n_meta)
        kk = k_aug(0, ROW_OFF, k_const[0:ROW_OFF])
        col = lax.broadcasted_iota(jnp.int32, (rows, ROW_OFF), 1)
        vv = _v_ones(v_ref, 0, ROW_OFF)
        fix = above_diag(ROW_OFF, ROW_OFF) if first else None
        m, acc = [], []
        for c in range(2):
            s = _dot_nt(qa[c], kk)
            if first:
                s = s - fix
            s = jnp.where(col >= N_PAD, s, NEG_BIG)
            mc = jnp.max(s, axis=1, keepdims=True)
            m.append(mc)
            acc.append(_dot(jnp.exp2(s - mc).astype(BF16), vv))

        for rel in range(n_chunks):
            diag = rel < per_tile
            if first:
                sgn = 1.0 if diag else -1.0
                n_hi = float(-(rel * tk) // BIAS_SPLIT)
                k0 = ROW_OFF + rel * tk
            else:
                raw = t * per_tile + rel
                wrapped = raw >= n_chunks
                c_idx = jnp.where(wrapped, raw - n_chunks, raw)
                sgn = 1.0 if diag else jnp.where(wrapped, 1.0, -1.0)
                n_hi = ((t * per_tile - c_idx) * (tk // BIAS_SPLIT)).astype(F32)
                k0 = pl.multiple_of(ROW_OFF + c_idx * tk, 128)
            qa = q_aug(sgn, n_hi)
            kk = k_aug(k0, tk, k_const)
            vv = _v_ones(v_ref, k0, tk)
            fix = above_diag(tk, -rel * tk) if diag else None
            for c in range(2):
                s = _dot_nt(qa[c], kk)
                if diag:
                    s = s - fix
                m_new = jnp.maximum(m[c], jnp.max(s, axis=1, keepdims=True))
                p = jnp.exp2(s - m_new)
                acc[c] = jnp.exp2(m[c] - m_new) * acc[c] + _dot(p.astype(BF16), vv)
                m[c] = m_new
        o = acc[0][:, :dv] / acc[0][:, dv:] - lam * (acc[1][:, :dv] / acc[1][:, dv:])
        o = _rms(o, g) * (1.0 - lambda_init)
        o_ref[pl.ds(r0, rows), :] = o.astype(BF16)

    o_ref[0:N_PAD, :] = jnp.zeros((N_PAD, o_ref.shape[1]), o_ref.dtype)
    q_tile(N_PAD, N_META + tq, 0)

    def body(i, carry):
        q_tile(pl.multiple_of(ROW_OFF + i * tq, 128), tq, i)
        return carry

    lax.fori_loop(1, seq // tq, body, 0)


def _diff_attn(slopes, q, k, v, lam_p, g, seq, lambda_init):
    b, lp, _ = q.shape
    tq = _pick_tile(seq, 512)
    tk = _pick_tile(seq, 256)
    assert tk % BIAS_SPLIT == 0 and tk <= 256 and (lp + tq) // BIAS_SPLIT < 256
    w = 2 * DIFF_HEAD_DIM
    blk = pl.BlockSpec((None, lp, w), lambda i, j: (i, 0, j))
    return pl.pallas_call(
        functools.partial(_diff_attn_kernel, seq=seq, tq=tq, tk=tk, lambda_init=lambda_init),
        grid=(b, DIFF_HEADS),
        in_specs=[pl.BlockSpec(memory_space=pltpu.SMEM), blk, blk, blk,
                  _full(lam_p.shape), _full(g.shape)],
        out_specs=blk,
        out_shape=jax.ShapeDtypeStruct((b, lp, DIFF_HEADS * w), BF16),
        compiler_params=_cparams(2),
        name="diff_attn",
    )(slopes, q, k, v, lam_p, g)


def _gelu_tanh(x):
    c = math.sqrt(2.0 / math.pi)
    return x * (0.5 * (1.0 + jnp.tanh(c * (x + 0.044715 * (x * x * x)))))


def _lru_prep_kernel(h_ref, g_ref, w_ref, xb_ref, gate_ref):
    u = _rms(h_ref[...], g_ref[...]).astype(BF16)
    for c in range(LRU_WIDTH // 256):
        sl = slice(c * 256, (c + 1) * 256)
        xb_ref[:, sl] = _dot(u, w_ref[:, c * 256:(c + 1) * 256])
        gate = _dot(u, w_ref[:, LRU_WIDTH + c * 256:LRU_WIDTH + (c + 1) * 256])
        gate_ref[:, sl] = _gelu_tanh(gate).astype(BF16)


def _lru_prep(h, g, w, j):
    rows = h.shape[0]
    tm = PREP_TILE if rows % PREP_TILE == 0 else ROW_TILE
    row = lambda i: (i, 0)
    return pl.pallas_call(
        _lru_prep_kernel,
        grid=(rows // tm,),
        in_specs=[pl.BlockSpec((tm, D_MODEL), row), _full(g.shape), _layer(w, j)],
        out_specs=[pl.BlockSpec((tm, LRU_WIDTH), row)] * 2,
        out_shape=[jax.ShapeDtypeStruct((rows, LRU_WIDTH), F32),
                   jax.ShapeDtypeStruct((rows, LRU_WIDTH), BF16)],
        compiler_params=_cparams(1),
        name="lru_prep",
    )(h, g, w)


def _softplus(x):
    return jnp.maximum(x, 0.0) + jnp.log(1.0 + jnp.exp(-jnp.abs(x)))


def _group_scan(a, u, carry, reverse):
    row = lax.broadcasted_iota(jnp.int32, a.shape, 0)
    for k in (1, 2, 4):
        shift = (8 - k) if reverse else k
        keep = (row < 8 - k) if reverse else (row >= k)
        a_s = pltpu.roll(a, shift, 0)
        u_s = pltpu.roll(u, shift, 0)
        u = jnp.where(keep, a * u_s + u, u)
        a = jnp.where(keep, a * a_s, a)
    return a * carry + u


def _odd_part(n):
    while n % 2 == 0:
        n //= 2
    return n


def _lru_scan_kernel(xa_ref, xh_ref, gate_ref, cw_ref, cb_ref, wg_ref, bg_ref, lam_ref, o_ref,
                     xc_ref, y_ref, yb_ref, *, lp, seg):
    ch = LRU_CHUNK
    width = 2 * HALF
    sc = 8 * seg
    n_chunks = lp // sc
    x_refs = (xa_ref, xh_ref)

    def strided(t0, j):
        return pl.ds(t0 + j, 8, stride=seg)

    row8 = lax.broadcasted_iota(jnp.int32, (8, HALF), 0)
    for c in range(n_chunks):
        t0 = c * sc
        for hh in range(2):
            lanes = slice(hh * HALF, (hh + 1) * HALF)
            cw = cw_ref[:, lanes]
            cb = cb_ref[:, lanes]
            xs = {j: x_refs[hh][strided(t0, j), :] for j in range(-2 if c else 0, seg + (1 if c < n_chunks - 1 else 0))}
            if c == 0:
                for j in (-2, -1):
                    xs[j] = jnp.where(row8 >= 1, pltpu.roll(xs[seg + j], 1, 0), 0.0)
            if c == n_chunks - 1:
                xs[seg] = jnp.where(row8 <= 6, pltpu.roll(xs[0], 7, 0), 0.0)
            for j in range(seg):
                acc = cb + cw[0:1] * xs[j - 2] + cw[1:2] * xs[j - 1] + cw[2:3] * xs[j] + cw[3:4] * xs[j + 1]
                if t0 + j < N_PAD:
                    acc = jnp.where(row8 * seg + (t0 + j) >= N_PAD, acc, 0.0)
                xc_ref[t0 + 8 * j:t0 + 8 * j + 8, lanes] = acc

    def gates(xc, d, rate):
        xcb = xc.astype(BF16)
        r = jax.nn.sigmoid(_dot(xcb, wg_ref[d, 0]) + bg_ref[2 * d:2 * d + 1, :])
        i = jax.nn.sigmoid(_dot(xcb, wg_ref[d, 1]) + bg_ref[2 * d + 1:2 * d + 2, :])
        a = jnp.exp2(r * rate)
        x = 1.0 - a * a
        u = jnp.where(x > 0.0, x * lax.rsqrt(x), 0.0) * (i * xc)
        return a, u

    def scan_chunk(a, u, carry, reverse):
        hs, ps = [None] * seg, [None] * seg
        h = p = None
        for j in (reversed(range(seg)) if reverse else range(seg)):
            aj, uj = a[8 * j:8 * j + 8], u[8 * j:8 * j + 8]
            h = uj if h is None else aj * h + uj
            p = aj if p is None else aj * p
            hs[j], ps[j] = h, p
        ends = _group_scan(p, h, carry, reverse)
        row = lax.broadcasted_iota(jnp.int32, ends.shape, 0)
        if reverse:
            entering = jnp.where(row == 7, carry, pltpu.roll(ends, 7, 0))
            carry = ends[0:1]
        else:
            entering = jnp.where(row == 0, carry, pltpu.roll(ends, 1, 0))
            carry = ends[7:8]
        return [hs[j] + ps[j] * entering for j in range(seg)], carry

    def rate(d):
        return _softplus(-lam_ref[d:d + 1, :]) * (-LRU_C * LOG2E)

    rates = (rate(0), rate(1))

    def put_perm(ref, t0, ys):
        for j in range(seg):
            ref[0, strided(t0, j), :] = ys[j][:, :HALF]
            ref[1, strided(t0, j), :] = ys[j][:, HALF:]

    def both(c, carries):
        cf, cbk = carries
        tf = pl.multiple_of(c * sc, 8)
        tb = pl.multiple_of((n_chunks - 1 - c) * sc, 8)
        af, uf = gates(xc_ref[pl.ds(tf, sc), :], 0, rates[0])
        ab, ub = gates(xc_ref[pl.ds(tb, sc), :], 1, rates[1])
        yf, cf = scan_chunk(af, uf, cf, False)
        yb, cbk = scan_chunk(ab, ub, cbk, True)
        put_perm(y_ref, tf, yf)
        put_perm(yb_ref, tb, yb)
        return cf, cbk

    zero = jnp.zeros((1, width), F32)
    lax.fori_loop(0, n_chunks, both, (zero, zero), unroll=LRU_UNROLL)

    def finish(c, carry):
        rows = pl.ds(pl.multiple_of(c * ch, 128), ch)
        y = jnp.concatenate([y_ref[0, rows, :] + yb_ref[0, rows, :],
                             y_ref[1, rows, :] + yb_ref[1, rows, :]], axis=1)
        o_ref[rows, :] = (y * gate_ref[rows, :].astype(F32)).astype(BF16)
        return carry

    lax.fori_loop(0, lp // ch, finish, 0)


def _lru_scan(xb, gate, cw, cb, wg, bg, lam):
    b, lp, _ = xb.shape
    w = LRU_BLOCK_W
    seg = _odd_part(lp // 8)
    blk = pl.BlockSpec((None, lp, w), lambda i, j: (i, 0, j))
    half = lambda k: pl.BlockSpec((None, lp, HALF), lambda i, j: (i, 0, 2 * j + k))
    col = lambda r: pl.BlockSpec((r, w), lambda i, j: (0, j))
    return pl.pallas_call(
        functools.partial(_lru_scan_kernel, lp=lp, seg=seg),
        grid=(b, LRU_BLOCKS),
        in_specs=[half(0), half(1), blk, col(4), col(1),
                  pl.BlockSpec((2, 2, None, w, w), lambda i, j: (0, 0, j, 0, 0)),
                  col(4), col(2)],
        out_specs=blk,
        out_shape=jax.ShapeDtypeStruct((b, lp, LRU_WIDTH), BF16),
        scratch_shapes=[pltpu.VMEM((lp, w), F32), pltpu.VMEM((2, lp, HALF), F32), pltpu.VMEM((2, lp, HALF), F32)],
        compiler_params=_cparams(2),
        name="lru_scan",
    )(xb, xb, gate, cw, cb, wg, bg, lam)


def _post_kernel(o_ref, h_ref, wo_ref, g_ref, win_ref, wout_ref, out_ref):
    g = g_ref[...]
    x = h_ref[...] + _rms(_dot(o_ref[...], wo_ref[...]), g[1:2])
    xn = _rms(x, g[2:3]).astype(BF16)
    gg = _dot(xn, win_ref[:, :FFN_HIDDEN])
    uu = _dot(xn, win_ref[:, FFN_HIDDEN:])
    a = (gg * jax.nn.sigmoid(gg) * uu).astype(BF16)
    out_ref[...] = x + _rms(_dot(a, wout_ref[...]), g[3:4])


def _post(o, h, w_o, j, g, w_in, w_out, i, final=None):
    rows = h.shape[0]
    once = dict(pipeline_mode=pl.Buffered(1))
    weights = [_layer(w_o, j, **once), _full(g.shape), _layer(w_in, i, **once), _layer(w_out, i, **once)]
    if final is None:
        tm = POST_TILE if rows % POST_TILE == 0 else ROW_TILE
        grid = (rows // tm,)
        row = lambda r: (r, 0)
        acts = [pl.BlockSpec((tm, o.shape[1]), row), pl.BlockSpec((tm, D_MODEL), row)]
        out_rows, out_spec = rows, pl.BlockSpec((tm, D_MODEL), row)
    else:
        b, lp, seq = final
        tm = _pick_tile(seq, POST_TILE)
        per = seq // tm
        grid = (b, per)
        start = lambda bi, t: (pl.multiple_of(bi * lp + ROW_OFF + t * tm, 128), 0)
        acts = [pl.BlockSpec((pl.Element(tm), pl.Element(o.shape[1])), start),
                pl.BlockSpec((pl.Element(tm), pl.Element(D_MODEL)), start)]
        out_rows, out_spec = b * seq, pl.BlockSpec((tm, D_MODEL), lambda bi, t: (bi * per + t, 0))
    return pl.pallas_call(
        _post_kernel,
        grid=grid,
        in_specs=acts + weights,
        out_specs=out_spec,
        out_shape=jax.ShapeDtypeStruct((out_rows, D_MODEL), F32),
        compiler_params=_cparams(len(grid)),
        name="post_ffn",
    )(o, h, w_o, g, w_in, w_out)


def _rope_rows(lp):
    pos = jnp.arange(lp, dtype=F32) - float(N_PAD)
    inv_freq = ROPE_THETA ** (-jnp.arange(0, MLA_ROPE, 2, dtype=F32) / MLA_ROPE)
    ang = pos[:, None] * inv_freq[None, :]
    c, s = jnp.cos(ang), jnp.sin(ang)
    z64 = jnp.zeros((lp, 64), F32)
    return jnp.concatenate([c, c, z64], axis=1), jnp.concatenate([s, s, z64], axis=1)


def _with_swapped(rope):
    half = rope.shape[-1] // 2
    return jnp.concatenate([rope, -rope[..., half:], rope[..., :half]], axis=-1)


def kernel(x, meta_tokens, norm_g, mla_w_in, mla_q_norm, mla_kv_norm, mla_w_uq, mla_w_ukv, mla_w_o,
           diff_w_in, diff_lambda, diff_subln, diff_w_o, lru_w_in, lru_conv_w, lru_conv_b,
           lru_w_gates, lru_b_gates, lru_lambda, lru_w_o, ffn_w_in, ffn_w_out):
    b, seq, d = x.shape
    depth = norm_g.shape[0]
    lp = seq + ROW_OFF
    assert d == D_MODEL and seq % 128 == 0 and lp % LRU_CHUNK == 0 and lp % ROW_TILE == 0

    meta = jnp.broadcast_to(meta_tokens[None].astype(x.dtype), (b, N_META, d))
    h = jnp.concatenate([jnp.zeros((b, N_PAD, d), x.dtype), meta, x], axis=1).reshape(b * lp, d)
    cos, sin = _rope_rows(lp)
    slopes = 2.0 ** (-8.0 * jnp.arange(1, DIFF_HEADS + 1, dtype=F32) / DIFF_HEADS)

    n_mla = mla_w_in.shape[0]
    lat = MLA_Q_RANK + MLA_KV_RANK
    m_in = jnp.concatenate([mla_w_in[..., :lat], _with_swapped(mla_w_in[..., lat:])], axis=-1).astype(BF16)
    uq = mla_w_uq.reshape(n_mla, MLA_Q_RANK, MLA_HEADS, MLA_NOPE + MLA_ROPE)
    uq = jnp.concatenate([uq[..., :MLA_NOPE], _with_swapped(uq[..., MLA_NOPE:])], axis=-1)
    m_uq = uq.reshape(n_mla, MLA_Q_RANK, MLA_HEADS * MLA_QK).astype(BF16)
    ukv = mla_w_ukv.reshape(n_mla, MLA_KV_RANK, MLA_HEADS, MLA_NOPE + MLA_V)
    m_uk = ukv[..., :MLA_NOPE].reshape(n_mla, MLA_KV_RANK, MLA_HEADS * MLA_NOPE).astype(BF16)
    m_uv = ukv[..., MLA_NOPE:].reshape(n_mla, MLA_KV_RANK, MLA_HEADS * MLA_V).astype(BF16)
    m_qg, m_kvg = mla_q_norm[:, None, :], mla_kv_norm[:, None, :]
    w_o_stacks = (mla_w_o.astype(BF16), diff_w_o.astype(BF16), lru_w_o.astype(BF16))
    d_in, l_in = diff_w_in.astype(BF16), lru_w_in.astype(BF16)
    f_in, f_out = ffn_w_in.astype(BF16), ffn_w_out.astype(BF16)

    for i in range(depth):
        kind, j = i % 3, i // 3
        g = norm_g[i]
        if kind == 0:
            q, kn, kr, v = _mla_prep(h, g[0:1], j, m_in, m_qg, m_kvg, m_uq, m_uk, m_uv, cos, sin, lp)
            o = _mla_attn(q.reshape(b, lp, -1), kn.reshape(b, lp, -1), kr.reshape(b, lp, -1),
                          v.reshape(b, lp, -1), seq)
        elif kind == 1:
            lambda_init = 0.8 - 0.6 * math.exp(-0.3 * i)
            q, k, v = _diff_prep(h, g[0:1], d_in, j)
            o = _diff_attn(slopes, q.reshape(b, lp, -1), k.reshape(b, lp, -1), v.reshape(b, lp, -1),
                           diff_lambda[j], diff_subln[j][None], seq, lambda_init)
        else:
            xb, gate = _lru_prep(h, g[0:1], l_in, j)
            o = _lru_scan(xb.reshape(b, lp, -1), gate.reshape(b, lp, -1), lru_conv_w[j],
                          lru_conv_b[j][None], lru_w_gates[j].astype(BF16),
                          lru_b_gates[j].reshape(4, LRU_WIDTH), lru_lambda[j])
        last = (b, lp, seq) if i == depth - 1 else None
        h = _post(o.reshape(b * lp, -1), h, w_o_stacks[kind], j, g, f_in, f_out, i, final=last)
    return h.reshape(b, seq, d)
```

```python
import functools
import math

import jax
import jax.numpy as jnp
from jax import lax
from jax.experimental import pallas as pl
from jax.experimental.pallas import tpu as pltpu

F32 = jnp.float32
BF16 = jnp.bfloat16

D_MODEL = 1024
N_META = 16
ROW_OFF = 128
N_PAD = ROW_OFF - N_META
NORM_EPS = 1e-6
LOG2E = math.log2(math.e)
NEG_BIG = -1e30

MLA_HEADS = 16
MLA_Q_RANK = 384
MLA_KV_RANK = 256
MLA_NOPE = 128
MLA_ROPE = 64
MLA_V = 128
MLA_QK = 256
ROPE_THETA = 10000.0

DIFF_HEADS = 8
DIFF_HEAD_DIM = 64

LRU_WIDTH = 1536
LRU_BLOCKS = 6
LRU_BLOCK_W = 256
LRU_C = 8.0
LRU_CHUNK = 384
HALF = 128
LRU_UNROLL = 4

FFN_HIDDEN = 2816

ROW_TILE = 384
PREP_TILE = 768
POST_TILE = 512
VMEM_LIMIT = 56 * 1024 * 1024


def _cparams(n_axes):
    return pltpu.CompilerParams(dimension_semantics=("arbitrary",) * n_axes,
                                vmem_limit_bytes=VMEM_LIMIT)


def _rms(x, g):
    return x * lax.rsqrt(jnp.mean(x * x, axis=-1, keepdims=True) + NORM_EPS) * g


def _dot(a, b):
    return jnp.dot(a, b, preferred_element_type=F32)


def _dot_nt(a, b):
    return lax.dot_general(a, b, (((1,), (1,)), ((), ())), preferred_element_type=F32)


def _full(shape):
    nd = len(shape)
    return pl.BlockSpec(shape, lambda *_: (0,) * nd)


def _layer(a, i, **kw):
    nd = a.ndim - 1
    return pl.BlockSpec((None,) + a.shape[1:], lambda *_: (i,) + (0,) * nd, **kw)


def _rope128(v, cos, sin):
    return v * cos + pltpu.roll(v, 64, 1) * sin


def _mla_prep_kernel(h_ref, g_ref, win_ref, qg_ref, kvg_ref, wuq_ref, wuk_ref, wuv_ref,
                     cos_ref, sin_ref, q_ref, kn_ref, kr_ref, v_ref, *, q_scale):
    u = _rms(h_ref[...], g_ref[...]).astype(BF16)
    t = _dot(u, win_ref[...])
    cq = _rms(t[:, :MLA_Q_RANK], qg_ref[...]).astype(BF16)
    ckv = _rms(t[:, MLA_Q_RANK:MLA_Q_RANK + MLA_KV_RANK], kvg_ref[...]).astype(BF16)
    cos, sin = cos_ref[...], sin_ref[...]
    kr_ref[...] = _rope128(t[:, MLA_Q_RANK + MLA_KV_RANK:], cos, sin).astype(BF16)
    for hp in range(MLA_HEADS // 2):
        sl = slice(hp * 256, (hp + 1) * 256)
        kn_ref[:, sl] = _dot(ckv, wuk_ref[:, sl]).astype(BF16)
        v_ref[:, sl] = _dot(ckv, wuv_ref[:, sl]).astype(BF16)
        for j in range(2):
            h = 2 * hp + j
            q = _dot(cq, wuq_ref[:, h * MLA_QK:(h + 1) * MLA_QK])
            q_ref[:, h * MLA_QK:h * MLA_QK + 128] = (q[:, :128] * q_scale).astype(BF16)
            q_ref[:, h * MLA_QK + 128:(h + 1) * MLA_QK] = (_rope128(q[:, 128:], cos, sin) * q_scale).astype(BF16)


def _mla_prep(h, g, j, w_in, qg, kvg, wuq, wuk, wuv, cos, sin, lp):
    rows = h.shape[0]
    tm = lp // 4 if lp % 64 == 0 else ROW_TILE
    tiles_per_seq = lp // tm
    row = lambda i: (i, 0)
    pos = lambda i: (i % tiles_per_seq, 0)
    q_scale = (MLA_NOPE + MLA_ROPE) ** -0.5 * LOG2E
    widths = (MLA_HEADS * MLA_QK, MLA_HEADS * MLA_NOPE, 128, MLA_HEADS * MLA_V)
    return pl.pallas_call(
        functools.partial(_mla_prep_kernel, q_scale=q_scale),
        grid=(rows // tm,),
        in_specs=[pl.BlockSpec((tm, D_MODEL), row), _full(g.shape), _layer(w_in, j),
                  _layer(qg, j), _layer(kvg, j), _layer(wuq, j), _layer(wuk, j),
                  _layer(wuv, j), pl.BlockSpec((tm, 128), pos), pl.BlockSpec((tm, 128), pos)],
        out_specs=[pl.BlockSpec((tm, n), row) for n in widths],
        out_shape=[jax.ShapeDtypeStruct((rows, n), BF16) for n in widths],
        compiler_params=_cparams(1),
        name="mla_prep",
    )(h, g, w_in, qg, kvg, wuq, wuk, wuv, cos, sin)


def _v_ones(v_ref, k0, n):
    return jnp.concatenate([v_ref[pl.ds(k0, n), :], jnp.ones((n, v_ref.shape[1]), BF16)], axis=1)


def _query_tiles(q_tile, o_ref, seq, tq):
    o_ref[0:N_PAD, :] = jnp.zeros((N_PAD, o_ref.shape[1]), o_ref.dtype)
    q_tile(N_PAD, N_META + tq)

    def body(i, carry):
        q_tile(pl.multiple_of(ROW_OFF + i * tq, 128), tq)
        return carry

    lax.fori_loop(1, seq // tq, body, 0)


def _pick_tile(seq, largest):
    t = largest
    while seq % t:
        t //= 2
    return t


def _mla_attn_kernel(q_ref, kn_ref, kr_ref, v_ref, o_ref, *, seq, tq, tk):
    dv = v_ref.shape[1]

    def keys(k0, n):
        return jnp.concatenate([kn_ref[k0:k0 + n, :], kr_ref[k0:k0 + n, :]], axis=1)

    def q_tile(r0, rows):
        q = q_ref[pl.ds(r0, rows), :]
        s = _dot_nt(q, keys(0, ROW_OFF))
        col = lax.broadcasted_iota(jnp.int32, (rows, ROW_OFF), 1)
        s = jnp.where(col >= N_PAD, s, NEG_BIG)
        m = jnp.max(s, axis=1, keepdims=True)
        acc = _dot(jnp.exp2(s - m).astype(BF16), _v_ones(v_ref, 0, ROW_OFF))
        for c in range(seq // tk):
            k0 = ROW_OFF + c * tk
            s = _dot_nt(q, keys(k0, tk))
            m_new = jnp.maximum(m, jnp.max(s, axis=1, keepdims=True))
            p = jnp.exp2(s - m_new)
            acc = jnp.exp2(m - m_new) * acc + _dot(p.astype(BF16), _v_ones(v_ref, k0, tk))
            m = m_new
        o_ref[pl.ds(r0, rows), :] = (acc[:, :dv] / acc[:, dv:]).astype(BF16)

    _query_tiles(q_tile, o_ref, seq, tq)


def _mla_attn(q, kn, kr, v, seq):
    b, lp, _ = q.shape
    tq = _pick_tile(seq, 4096)
    tk = _pick_tile(seq, 256)
    head = lambda w: pl.BlockSpec((None, lp, w), lambda i, j: (i, 0, j))
    return pl.pallas_call(
        functools.partial(_mla_attn_kernel, seq=seq, tq=tq, tk=tk),
        grid=(b, MLA_HEADS),
        in_specs=[head(MLA_QK), head(MLA_NOPE), pl.BlockSpec((None, lp, 128), lambda i, j: (i, 0, 0)),
                  head(MLA_V)],
        out_specs=head(MLA_V),
        out_shape=jax.ShapeDtypeStruct((b, lp, MLA_HEADS * MLA_V), BF16),
        compiler_params=_cparams(2),
        name="mla_attn",
    )(q, kn, kr, v)


def _diff_prep_kernel(h_ref, g_ref, w_ref, q_ref, k_ref, v_ref, *, q_scale):
    u = _rms(h_ref[...], g_ref[...]).astype(BF16)
    n = q_ref.shape[1]
    for c in range(n // 256):
        sl = slice(c * 256, (c + 1) * 256)
        q_ref[:, sl] = (_dot(u, w_ref[:, c * 256:(c + 1) * 256]) * q_scale).astype(BF16)
        k_ref[:, sl] = _dot(u, w_ref[:, n + c * 256:n + (c + 1) * 256]).astype(BF16)
        v_ref[:, sl] = _dot(u, w_ref[:, 2 * n + c * 256:2 * n + (c + 1) * 256]).astype(BF16)


def _diff_prep(h, g, w, j):
    rows = h.shape[0]
    tm = PREP_TILE if rows % PREP_TILE == 0 else ROW_TILE
    n = w.shape[2] // 3
    row = lambda i: (i, 0)
    return pl.pallas_call(
        functools.partial(_diff_prep_kernel, q_scale=DIFF_HEAD_DIM ** -0.5 * LOG2E),
        grid=(rows // tm,),
        in_specs=[pl.BlockSpec((tm, D_MODEL), row), _full(g.shape), _layer(w, j)],
        out_specs=[pl.BlockSpec((tm, n), row)] * 3,
        out_shape=[jax.ShapeDtypeStruct((rows, n), BF16)] * 3,
        compiler_params=_cparams(1),
        name="diff_prep",
    )(h, g, w)


BIAS_SPLIT = 32


def _split3(x):
    c1 = x.astype(BF16).astype(F32)
    c2 = (x - c1).astype(BF16).astype(F32)
    c3 = (x - c1 - c2).astype(BF16).astype(F32)
    return c1, c2, c3


def _diff_attn_kernel(slope_ref, q_ref, k_ref, v_ref, lam_ref, g_ref, o_ref, *,
                      seq, tq, tk, lambda_init):
    slope2 = slope_ref[pl.program_id(1)] * LOG2E
    lp4 = lam_ref[...]
    lam = (jnp.exp(jnp.sum(lp4[0:1] * lp4[1:2], axis=1, keepdims=True))
           - jnp.exp(jnp.sum(lp4[2:3] * lp4[3:4], axis=1, keepdims=True)) + lambda_init)
    g = g_ref[...]
    dv = v_ref.shape[1]
    n_chunks = seq // tk
    per_tile = tq // tk

    def lanes3(lane, x1, x2, x3):
        return jnp.where(lane % 3 == 0, x1, jnp.where(lane % 3 == 1, x2, x3))

    c1, c2, c3 = _split3(jnp.full((1, 128), slope2, F32))
    lane_k = lax.broadcasted_iota(jnp.int32, (tk, 128), 1)
    jj = lax.broadcasted_iota(jnp.int32, (tk, 128), 0).astype(F32)
    piece = lanes3(lane_k, c1, c2, c3)
    k_const = jnp.where(lane_k < 3, jj,
                        jnp.where(lane_k < 6, float(BIAS_SPLIT) * piece,
                                  jnp.where(lane_k < 9, piece, 0.0))).astype(BF16)

    def k_aug(k0, n, const):
        return jnp.concatenate([k_ref[pl.ds(k0, n), :], const], axis=1)

    def bias_rows(rows, first):
        ii = lax.broadcasted_iota(jnp.int32, (rows, 128), 0)
        if first:
            ii = ii - N_META
        lane_q = lax.broadcasted_iota(jnp.int32, (rows, 128), 1)
        in_a = (lane_q >= 3) & (lane_q < 6)
        base = jnp.where(lane_q < 3, lanes3(lane_q, c1, c2, c3),
                         jnp.where(in_a, -(ii >> 5).astype(F32),
                                   jnp.where(lane_q < 9, -(ii & (BIAS_SPLIT - 1)).astype(F32), 0.0))
                         ).astype(BF16)
        return base, jnp.where(in_a, -1.0, 0.0).astype(BF16)

    def above_diag(rows, width, off, first):
        d = (lax.broadcasted_iota(jnp.int32, (rows, width), 1)
             - lax.broadcasted_iota(jnp.int32, (rows, width), 0))
        if first:
            d = d + N_META
        return jnp.maximum((d - off).astype(F32), 0.0) * (2.0 * slope2)

    def q_tile(r0, rows, t, base, unit, fixes):
        first = isinstance(t, int)
        q = q_ref[pl.ds(r0, rows), :]
        lane = lax.broadcasted_iota(jnp.int32, q.shape, 1)
        zero = jnp.zeros_like(q)
        qs = (jnp.where(lane < DIFF_HEAD_DIM, q, zero), jnp.where(lane >= DIFF_HEAD_DIM, q, zero))

        def q_aug(sgn, n_hi):
            tt = base + unit * (n_hi if isinstance(n_hi, float) else n_hi.astype(BF16))
            if not isinstance(sgn, float):
                tt = tt * sgn.astype(BF16)
            elif sgn != 1.0:
                tt = tt * sgn
            tt = tt.astype(BF16)
            return [jnp.concatenate([qs[c], tt], axis=1) for c in range(2)]

        base_row = ROW_OFF + t * tq
        n_meta = float(base_row // BIAS_SPLIT) if first else (base_row // BIAS_SPLIT).astype(F32)
        qa = q_aug(1.0, n_meta)
        kk = k_aug(0, ROW_OFF, k_const[0:ROW_OFF])
        col = lax.broadcasted_iota(jnp.int32, (rows, ROW_OFF), 1)
        vv = _v_ones(v_ref, 0, ROW_OFF)
        fix = above_diag(rows, ROW_OFF, ROW_OFF, True) if first else None
        m, acc = [], []
        for c in range(2):
            s = _dot_nt(qa[c], kk)
            if first:
                s = s - fix
            s = jnp.where(col >= N_PAD, s, NEG_BIG)
            mc = jnp.max(s, axis=1, keepdims=True)
            m.append(mc)
            acc.append(_dot(jnp.exp2(s - mc).astype(BF16), vv))

        for rel in range(n_chunks):
            diag = rel < per_tile
            if first:
                sgn = 1.0 if diag else -1.0
                n_hi = float(-(rel * tk) // BIAS_SPLIT)
                k0 = ROW_OFF + rel * tk
            else:
                raw = t * per_tile + rel
                wrapped = raw >= n_chunks
                c_idx = jnp.where(wrapped, raw - n_chunks, raw)
                sgn = 1.0 if diag else jnp.where(wrapped, 1.0, -1.0)
                n_hi = ((t * per_tile - c_idx) * (tk // BIAS_SPLIT)).astype(F32)
                k0 = pl.multiple_of(ROW_OFF + c_idx * tk, 128)
            qa = q_aug(sgn, n_hi)
            kk = k_aug(k0, tk, k_const)
            vv = _v_ones(v_ref, k0, tk)
            fix = fixes[rel] if diag else None
            for c in range(2):
                s = _dot_nt(qa[c], kk)
                if diag:
                    s = s - fix
                m_new = jnp.maximum(m[c], jnp.max(s, axis=1, keepdims=True))
                p = jnp.exp2(s - m_new)
                acc[c] = jnp.exp2(m[c] - m_new) * acc[c] + _dot(p.astype(BF16), vv)
                m[c] = m_new
        o = acc[0][:, :dv] / acc[0][:, dv:] - lam * (acc[1][:, :dv] / acc[1][:, dv:])
        o = _rms(o, g) * (1.0 - lambda_init)
        o_ref[pl.ds(r0, rows), :] = o.astype(BF16)

    o_ref[0:N_PAD, :] = jnp.zeros((N_PAD, o_ref.shape[1]), o_ref.dtype)
    rows0 = N_META + tq
    q_tile(N_PAD, rows0, 0, *bias_rows(rows0, True),
           [above_diag(rows0, tk, -rel * tk, True) for rel in range(per_tile)])

    base, unit = bias_rows(tq, False)
    fixes = [above_diag(tq, tk, -rel * tk, False) for rel in range(per_tile)]

    def body(i, carry):
        q_tile(pl.multiple_of(ROW_OFF + i * tq, 128), tq, i, base, unit, fixes)
        return carry

    lax.fori_loop(1, seq // tq, body, 0, unroll=2)


def _diff_attn(slopes, q, k, v, lam_p, g, seq, lambda_init):
    b, lp, _ = q.shape
    tq = _pick_tile(seq, 512)
    tk = _pick_tile(seq, 256)
    assert tk % BIAS_SPLIT == 0 and tk <= 256 and (lp + tq) // BIAS_SPLIT < 256
    w = 2 * DIFF_HEAD_DIM
    blk = pl.BlockSpec((None, lp, w), lambda i, j: (i, 0, j))
    return pl.pallas_call(
        functools.partial(_diff_attn_kernel, seq=seq, tq=tq, tk=tk, lambda_init=lambda_init),
        grid=(b, DIFF_HEADS),
        in_specs=[pl.BlockSpec(memory_space=pltpu.SMEM), blk, blk, blk,
                  _full(lam_p.shape), _full(g.shape)],
        out_specs=blk,
        out_shape=jax.ShapeDtypeStruct((b, lp, DIFF_HEADS * w), BF16),
        compiler_params=_cparams(2),
        name="diff_attn",
    )(slopes, q, k, v, lam_p, g)


def _gelu_tanh(x):
    c = math.sqrt(2.0 / math.pi)
    return x * (0.5 * (1.0 + jnp.tanh(c * (x + 0.044715 * (x * x * x)))))


def _lru_prep_kernel(h_ref, g_ref, w_ref, xb_ref, gate_ref):
    u = _rms(h_ref[...], g_ref[...]).astype(BF16)
    for c in range(LRU_WIDTH // 256):
        sl = slice(c * 256, (c + 1) * 256)
        xb_ref[:, sl] = _dot(u, w_ref[:, c * 256:(c + 1) * 256])
        gate = _dot(u, w_ref[:, LRU_WIDTH + c * 256:LRU_WIDTH + (c + 1) * 256])
        gate_ref[:, sl] = _gelu_tanh(gate).astype(BF16)


def _lru_prep(h, g, w, j):
    rows = h.shape[0]
    tm = PREP_TILE if rows % PREP_TILE == 0 else ROW_TILE
    row = lambda i: (i, 0)
    return pl.pallas_call(
        _lru_prep_kernel,
        grid=(rows // tm,),
        in_specs=[pl.BlockSpec((tm, D_MODEL), row), _full(g.shape), _layer(w, j)],
        out_specs=[pl.BlockSpec((tm, LRU_WIDTH), row)] * 2,
        out_shape=[jax.ShapeDtypeStruct((rows, LRU_WIDTH), F32),
                   jax.ShapeDtypeStruct((rows, LRU_WIDTH), BF16)],
        compiler_params=_cparams(1),
        name="lru_prep",
    )(h, g, w)


def _softplus(x):
    return jnp.maximum(x, 0.0) + jnp.log(1.0 + jnp.exp(-jnp.abs(x)))


def _group_scan(a, u, carry, reverse):
    row = lax.broadcasted_iota(jnp.int32, a.shape, 0)
    for k in (1, 2, 4):
        shift = (8 - k) if reverse else k
        keep = (row < 8 - k) if reverse else (row >= k)
        a_s = pltpu.roll(a, shift, 0)
        u_s = pltpu.roll(u, shift, 0)
        u = jnp.where(keep, a * u_s + u, u)
        a = jnp.where(keep, a * a_s, a)
    return a * carry + u


def _odd_part(n):
    while n % 2 == 0:
        n //= 2
    return n


def _lru_scan_kernel(xa_ref, xh_ref, gate_ref, cw_ref, cb_ref, wg_ref, bg_ref, lam_ref, o_ref,
                     xc_ref, y_ref, yb_ref, *, lp, seg):
    ch = LRU_CHUNK
    width = 2 * HALF
    sc = 8 * seg
    n_chunks = lp // sc
    x_refs = (xa_ref, xh_ref)

    def strided(t0, j):
        return pl.ds(t0 + j, 8, stride=seg)

    row8 = lax.broadcasted_iota(jnp.int32, (8, HALF), 0)
    for c in range(n_chunks):
        t0 = c * sc
        for hh in range(2):
            lanes = slice(hh * HALF, (hh + 1) * HALF)
            cw = cw_ref[:, lanes]
            cb = cb_ref[:, lanes]
            xs = {j: x_refs[hh][strided(t0, j), :] for j in range(-2 if c else 0, seg + (1 if c < n_chunks - 1 else 0))}
            if c == 0:
                for j in (-2, -1):
                    xs[j] = jnp.where(row8 >= 1, pltpu.roll(xs[seg + j], 1, 0), 0.0)
            if c == n_chunks - 1:
                xs[seg] = jnp.where(row8 <= 6, pltpu.roll(xs[0], 7, 0), 0.0)
            for j in range(seg):
                acc = cb + cw[0:1] * xs[j - 2] + cw[1:2] * xs[j - 1] + cw[2:3] * xs[j] + cw[3:4] * xs[j + 1]
                if t0 + j < N_PAD:
                    acc = jnp.where(row8 * seg + (t0 + j) >= N_PAD, acc, 0.0)
                xc_ref[t0 + 8 * j:t0 + 8 * j + 8, lanes] = acc

    def gates(xc, d, rate):
        xcb = xc.astype(BF16)
        r = jax.nn.sigmoid(_dot(xcb, wg_ref[d, 0]) + bg_ref[2 * d:2 * d + 1, :])
        i = jax.nn.sigmoid(_dot(xcb, wg_ref[d, 1]) + bg_ref[2 * d + 1:2 * d + 2, :])
        a = jnp.exp2(r * rate)
        x = 1.0 - a * a
        u = jnp.where(x > 0.0, x * lax.rsqrt(x), 0.0) * (i * xc)
        return a, u

    def scan_chunk(a, u, carry, reverse):
        hs, ps = [None] * seg, [None] * seg
        h = p = None
        for j in (reversed(range(seg)) if reverse else range(seg)):
            aj, uj = a[8 * j:8 * j + 8], u[8 * j:8 * j + 8]
            h = uj if h is None else aj * h + uj
            p = aj if p is None else aj * p
            hs[j], ps[j] = h, p
        ends = _group_scan(p, h, carry, reverse)
        row = lax.broadcasted_iota(jnp.int32, ends.shape, 0)
        if reverse:
            entering = jnp.where(row == 7, carry, pltpu.roll(ends, 7, 0))
            carry = ends[0:1]
        else:
            entering = jnp.where(row == 0, carry, pltpu.roll(ends, 1, 0))
            carry = ends[7:8]
        return [hs[j] + ps[j] * entering for j in range(seg)], carry

    def rate(d):
        return _softplus(-lam_ref[d:d + 1, :]) * (-LRU_C * LOG2E)

    rates = (rate(0), rate(1))

    def put_perm(ref, t0, ys):
        for j in range(seg):
            ref[0, strided(t0, j), :] = ys[j][:, :HALF]
            ref[1, strided(t0, j), :] = ys[j][:, HALF:]

    def both(c, carries):
        cf, cbk = carries
        tf = pl.multiple_of(c * sc, 8)
        tb = pl.multiple_of((n_chunks - 1 - c) * sc, 8)
        af, uf = gates(xc_ref[pl.ds(tf, sc), :], 0, rates[0])
        ab, ub = gates(xc_ref[pl.ds(tb, sc), :], 1, rates[1])
        yf, cf = scan_chunk(af, uf, cf, False)
        yb, cbk = scan_chunk(ab, ub, cbk, True)
        put_perm(y_ref, tf, yf)
        put_perm(yb_ref, tb, yb)
        return cf, cbk

    zero = jnp.zeros((1, width), F32)
    lax.fori_loop(0, n_chunks, both, (zero, zero), unroll=LRU_UNROLL)

    def finish(c, carry):
        rows = pl.ds(pl.multiple_of(c * ch, 128), ch)
        y = jnp.concatenate([y_ref[0, rows, :] + yb_ref[0, rows, :],
                             y_ref[1, rows, :] + yb_ref[1, rows, :]], axis=1)
        o_ref[rows, :] = (y * gate_ref[rows, :].astype(F32)).astype(BF16)
        return carry

    lax.fori_loop(0, lp // ch, finish, 0)


def _lru_scan(xb, gate, cw, cb, wg, bg, lam):
    b, lp, _ = xb.shape
    w = LRU_BLOCK_W
    seg = _odd_part(lp // 8)
    blk = pl.BlockSpec((None, lp, w), lambda i, j: (i, 0, j))
    half = lambda k: pl.BlockSpec((None, lp, HALF), lambda i, j: (i, 0, 2 * j + k))
    col = lambda r: pl.BlockSpec((r, w), lambda i, j: (0, j))
    return pl.pallas_call(
        functools.partial(_lru_scan_kernel, lp=lp, seg=seg),
        grid=(b, LRU_BLOCKS),
        in_specs=[half(0), half(1), blk, col(4), col(1),
                  pl.BlockSpec((2, 2, None, w, w), lambda i, j: (0, 0, j, 0, 0)),
                  col(4), col(2)],
        out_specs=blk,
        out_shape=jax.ShapeDtypeStruct((b, lp, LRU_WIDTH), BF16),
        scratch_shapes=[pltpu.VMEM((lp, w), F32), pltpu.VMEM((2, lp, HALF), F32), pltpu.VMEM((2, lp, HALF), F32)],
        compiler_params=_cparams(2),
        name="lru_scan",
    )(xb, xb, gate, cw, cb, wg, bg, lam)


def _post_kernel(o_ref, h_ref, wo_ref, g_ref, win_ref, wout_ref, out_ref):
    g = g_ref[...]
    x = h_ref[...] + _rms(_dot(o_ref[...], wo_ref[...]), g[1:2])
    xn = _rms(x, g[2:3]).astype(BF16)
    gg = _dot(xn, win_ref[:, :FFN_HIDDEN])
    uu = _dot(xn, win_ref[:, FFN_HIDDEN:])
    a = (gg * jax.nn.sigmoid(gg) * uu).astype(BF16)
    out_ref[...] = x + _rms(_dot(a, wout_ref[...]), g[3:4])


def _post(o, h, w_o, j, g, w_in, w_out, i, final=None):
    rows = h.shape[0]
    once = dict(pipeline_mode=pl.Buffered(1))
    weights = [_layer(w_o, j, **once), _full(g.shape), _layer(w_in, i, **once), _layer(w_out, i, **once)]
    if final is None:
        tm = POST_TILE if rows % POST_TILE == 0 else ROW_TILE
        grid = (rows // tm,)
        row = lambda r: (r, 0)
        acts = [pl.BlockSpec((tm, o.shape[1]), row), pl.BlockSpec((tm, D_MODEL), row)]
        out_rows, out_spec = rows, pl.BlockSpec((tm, D_MODEL), row)
    else:
        b, lp, seq = final
        tm = _pick_tile(seq, POST_TILE)
        per = seq // tm
        grid = (b, per)
        start = lambda bi, t: (pl.multiple_of(bi * lp + ROW_OFF + t * tm, 128), 0)
        acts = [pl.BlockSpec((pl.Element(tm), pl.Element(o.shape[1])), start),
                pl.BlockSpec((pl.Element(tm), pl.Element(D_MODEL)), start)]
        out_rows, out_spec = b * seq, pl.BlockSpec((tm, D_MODEL), lambda bi, t: (bi * per + t, 0))
    return pl.pallas_call(
        _post_kernel,
        grid=grid,
        in_specs=acts + weights,
        out_specs=out_spec,
        out_shape=jax.ShapeDtypeStruct((out_rows, D_MODEL), F32),
        compiler_params=_cparams(len(grid)),
        name="post_ffn",
    )(o, h, w_o, g, w_in, w_out)


def _rope_rows(lp):
    pos = jnp.arange(lp, dtype=F32) - float(N_PAD)
    inv_freq = ROPE_THETA ** (-jnp.arange(0, MLA_ROPE, 2, dtype=F32) / MLA_ROPE)
    ang = pos[:, None] * inv_freq[None, :]
    c, s = jnp.cos(ang), jnp.sin(ang)
    z64 = jnp.zeros((lp, 64), F32)
    return jnp.concatenate([c, c, z64], axis=1), jnp.concatenate([s, s, z64], axis=1)


def _with_swapped(rope):
    half = rope.shape[-1] // 2
    return jnp.concatenate([rope, -rope[..., half:], rope[..., :half]], axis=-1)


def kernel(x, meta_tokens, norm_g, mla_w_in, mla_q_norm, mla_kv_norm, mla_w_uq, mla_w_ukv, mla_w_o,
           diff_w_in, diff_lambda, diff_subln, diff_w_o, lru_w_in, lru_conv_w, lru_conv_b,
           lru_w_gates, lru_b_gates, lru_lambda, lru_w_o, ffn_w_in, ffn_w_out):
    b, seq, d = x.shape
    depth = norm_g.shape[0]
    lp = seq + ROW_OFF
    assert d == D_MODEL and seq % 128 == 0 and lp % LRU_CHUNK == 0 and lp % ROW_TILE == 0

    meta = jnp.broadcast_to(meta_tokens[None].astype(x.dtype), (b, N_META, d))
    h = jnp.concatenate([jnp.zeros((b, N_PAD, d), x.dtype), meta, x], axis=1).reshape(b * lp, d)
    cos, sin = _rope_rows(lp)
    slopes = 2.0 ** (-8.0 * jnp.arange(1, DIFF_HEADS + 1, dtype=F32) / DIFF_HEADS)

    n_mla = mla_w_in.shape[0]
    lat = MLA_Q_RANK + MLA_KV_RANK
    m_in = jnp.concatenate([mla_w_in[..., :lat], _with_swapped(mla_w_in[..., lat:])], axis=-1).astype(BF16)
    uq = mla_w_uq.reshape(n_mla, MLA_Q_RANK, MLA_HEADS, MLA_NOPE + MLA_ROPE)
    uq = jnp.concatenate([uq[..., :MLA_NOPE], _with_swapped(uq[..., MLA_NOPE:])], axis=-1)
    m_uq = uq.reshape(n_mla, MLA_Q_RANK, MLA_HEADS * MLA_QK).astype(BF16)
    ukv = mla_w_ukv.reshape(n_mla, MLA_KV_RANK, MLA_HEADS, MLA_NOPE + MLA_V)
    m_uk = ukv[..., :MLA_NOPE].reshape(n_mla, MLA_KV_RANK, MLA_HEADS * MLA_NOPE).astype(BF16)
    m_uv = ukv[..., MLA_NOPE:].reshape(n_mla, MLA_KV_RANK, MLA_HEADS * MLA_V).astype(BF16)
    m_qg, m_kvg = mla_q_norm[:, None, :], mla_kv_norm[:, None, :]
    w_o_stacks = (mla_w_o.astype(BF16), diff_w_o.astype(BF16), lru_w_o.astype(BF16))
    d_in, l_in = diff_w_in.astype(BF16), lru_w_in.astype(BF16)
    f_in, f_out = ffn_w_in.astype(BF16), ffn_w_out.astype(BF16)

    for i in range(depth):
        kind, j = i % 3, i // 3
        g = norm_g[i]
        if kind == 0:
            q, kn, kr, v = _mla_prep(h, g[0:1], j, m_in, m_qg, m_kvg, m_uq, m_uk, m_uv, cos, sin, lp)
            o = _mla_attn(q.reshape(b, lp, -1), kn.reshape(b, lp, -1), kr.reshape(b, lp, -1),
                          v.reshape(b, lp, -1), seq)
        elif kind == 1:
            lambda_init = 0.8 - 0.6 * math.exp(-0.3 * i)
            q, k, v = _diff_prep(h, g[0:1], d_in, j)
            o = _diff_attn(slopes, q.reshape(b, lp, -1), k.reshape(b, lp, -1), v.reshape(b, lp, -1),
                           diff_lambda[j], diff_subln[j][None], seq, lambda_init)
        else:
            xb, gate = _lru_prep(h, g[0:1], l_in, j)
            o = _lru_scan(xb.reshape(b, lp, -1), gate.reshape(b, lp, -1), lru_conv_w[j],
                          lru_conv_b[j][None], lru_w_gates[j].astype(BF16),
                          lru_b_gates[j].reshape(4, LRU_WIDTH), lru_lambda[j])
        last = (b, lp, seq) if i == depth - 1 else None
        h = _post(o.reshape(b * lp, -1), h, w_o_stacks[kind], j, g, f_in, f_out, i, final=last)
    return h.reshape(b, seq, d)
```

```python
import functools
import math

import jax
import jax.numpy as jnp
from jax import lax
from jax.experimental import pallas as pl
from jax.experimental.pallas import tpu as pltpu

F32 = jnp.float32
BF16 = jnp.bfloat16

D_MODEL = 1024
N_META = 16
ROW_OFF = 128
N_PAD = ROW_OFF - N_META
NORM_EPS = 1e-6
LOG2E = math.log2(math.e)
NEG_BIG = -1e30

MLA_HEADS = 16
MLA_Q_RANK = 384
MLA_KV_RANK = 256
MLA_NOPE = 128
MLA_ROPE = 64
MLA_V = 128
MLA_QK = 256
ROPE_THETA = 10000.0

DIFF_HEADS = 8
DIFF_HEAD_DIM = 64

LRU_WIDTH = 1536
LRU_BLOCKS = 6
LRU_BLOCK_W = 256
LRU_C = 8.0
LRU_CHUNK = 384
HALF = 128
LRU_UNROLL = 4

FFN_HIDDEN = 2816

ROW_TILE = 384
PREP_TILE = 768
POST_TILE = 512
VMEM_LIMIT = 56 * 1024 * 1024


def _cparams(n_axes):
    return pltpu.CompilerParams(dimension_semantics=("arbitrary",) * n_axes,
                                vmem_limit_bytes=VMEM_LIMIT)


def _rms(x, g):
    return x * lax.rsqrt(jnp.mean(x * x, axis=-1, keepdims=True) + NORM_EPS) * g


def _dot(a, b):
    return jnp.dot(a, b, preferred_element_type=F32)


def _dot_nt(a, b):
    return lax.dot_general(a, b, (((1,), (1,)), ((), ())), preferred_element_type=F32)


def _full(shape):
    nd = len(shape)
    return pl.BlockSpec(shape, lambda *_: (0,) * nd)


def _layer(a, i, **kw):
    nd = a.ndim - 1
    return pl.BlockSpec((None,) + a.shape[1:], lambda *_: (i,) + (0,) * nd, **kw)


def _rope128(v, cos, sin):
    return v * cos + pltpu.roll(v, 64, 1) * sin


def _mla_prep_kernel(h_ref, g_ref, win_ref, qg_ref, kvg_ref, wuq_ref, wuk_ref, wuv_ref,
                     cos_ref, sin_ref, q_ref, kn_ref, kr_ref, v_ref, *, q_scale):
    u = _rms(h_ref[...], g_ref[...]).astype(BF16)
    t = _dot(u, win_ref[...])
    cq = _rms(t[:, :MLA_Q_RANK], qg_ref[...]).astype(BF16)
    ckv = _rms(t[:, MLA_Q_RANK:MLA_Q_RANK + MLA_KV_RANK], kvg_ref[...]).astype(BF16)
    cos, sin = cos_ref[...], sin_ref[...]
    kr_ref[...] = _rope128(t[:, MLA_Q_RANK + MLA_KV_RANK:], cos, sin).astype(BF16)
    for hp in range(MLA_HEADS // 2):
        sl = slice(hp * 256, (hp + 1) * 256)
        kn_ref[:, sl] = _dot(ckv, wuk_ref[:, sl]).astype(BF16)
        v_ref[:, sl] = _dot(ckv, wuv_ref[:, sl]).astype(BF16)
        for j in range(2):
            h = 2 * hp + j
            q = _dot(cq, wuq_ref[:, h * MLA_QK:(h + 1) * MLA_QK])
            q_ref[:, h * MLA_QK:h * MLA_QK + 128] = (q[:, :128] * q_scale).astype(BF16)
            q_ref[:, h * MLA_QK + 128:(h + 1) * MLA_QK] = (_rope128(q[:, 128:], cos, sin) * q_scale).astype(BF16)


def _mla_prep(h, g, j, w_in, qg, kvg, wuq, wuk, wuv, cos, sin, lp):
    rows = h.shape[0]
    tm = lp // 4 if lp % 64 == 0 else ROW_TILE
    tiles_per_seq = lp // tm
    row = lambda i: (i, 0)
    pos = lambda i: (i % tiles_per_seq, 0)
    q_scale = (MLA_NOPE + MLA_ROPE) ** -0.5 * LOG2E
    widths = (MLA_HEADS * MLA_QK, MLA_HEADS * MLA_NOPE, 128, MLA_HEADS * MLA_V)
    return pl.pallas_call(
        functools.partial(_mla_prep_kernel, q_scale=q_scale),
        grid=(rows // tm,),
        in_specs=[pl.BlockSpec((tm, D_MODEL), row), _full(g.shape), _layer(w_in, j),
                  _layer(qg, j), _layer(kvg, j), _layer(wuq, j), _layer(wuk, j),
                  _layer(wuv, j), pl.BlockSpec((tm, 128), pos), pl.BlockSpec((tm, 128), pos)],
        out_specs=[pl.BlockSpec((tm, n), row) for n in widths],
        out_shape=[jax.ShapeDtypeStruct((rows, n), BF16) for n in widths],
        compiler_params=_cparams(1),
        name="mla_prep",
    )(h, g, w_in, qg, kvg, wuq, wuk, wuv, cos, sin)


def _v_ones(v_ref, k0, n):
    return jnp.concatenate([v_ref[pl.ds(k0, n), :], jnp.ones((n, v_ref.shape[1]), BF16)], axis=1)


def _query_tiles(q_tile, o_ref, seq, tq):
    o_ref[0:N_PAD, :] = jnp.zeros((N_PAD, o_ref.shape[1]), o_ref.dtype)
    q_tile(N_PAD, N_META + tq)

    def body(i, carry):
        q_tile(pl.multiple_of(ROW_OFF + i * tq, 128), tq)
        return carry

    lax.fori_loop(1, seq // tq, body, 0)


def _pick_tile(seq, largest):
    t = largest
    while seq % t:
        t //= 2
    return t


def _mla_attn_kernel(q_ref, kn_ref, kr_ref, v_ref, o_ref, *, seq, tq, tk):
    dv = v_ref.shape[1]

    def keys(k0, n):
        return jnp.concatenate([kn_ref[k0:k0 + n, :], kr_ref[k0:k0 + n, :]], axis=1)

    def q_tile(r0, rows):
        q = q_ref[pl.ds(r0, rows), :]
        s = _dot_nt(q, keys(0, ROW_OFF))
        col = lax.broadcasted_iota(jnp.int32, (rows, ROW_OFF), 1)
        s = jnp.where(col >= N_PAD, s, NEG_BIG)
        m = jnp.max(s, axis=1, keepdims=True)
        acc = _dot(jnp.exp2(s - m).astype(BF16), _v_ones(v_ref, 0, ROW_OFF))
        for c in range(seq // tk):
            k0 = ROW_OFF + c * tk
            s = _dot_nt(q, keys(k0, tk))
            m_new = jnp.maximum(m, jnp.max(s, axis=1, keepdims=True))
            p = jnp.exp2(s - m_new)
            acc = jnp.exp2(m - m_new) * acc + _dot(p.astype(BF16), _v_ones(v_ref, k0, tk))
            m = m_new
        o_ref[pl.ds(r0, rows), :] = (acc[:, :dv] / acc[:, dv:]).astype(BF16)

    _query_tiles(q_tile, o_ref, seq, tq)


def _mla_attn(q, kn, kr, v, seq):
    b, lp, _ = q.shape
    tq = _pick_tile(seq, 4096)
    tk = _pick_tile(seq, 256)
    head = lambda w: pl.BlockSpec((None, lp, w), lambda i, j: (i, 0, j))
    return pl.pallas_call(
        functools.partial(_mla_attn_kernel, seq=seq, tq=tq, tk=tk),
        grid=(b, MLA_HEADS),
        in_specs=[head(MLA_QK), head(MLA_NOPE), pl.BlockSpec((None, lp, 128), lambda i, j: (i, 0, 0)),
                  head(MLA_V)],
        out_specs=head(MLA_V),
        out_shape=jax.ShapeDtypeStruct((b, lp, MLA_HEADS * MLA_V), BF16),
        compiler_params=_cparams(2),
        name="mla_attn",
    )(q, kn, kr, v)


def _diff_prep_kernel(h_ref, g_ref, w_ref, q_ref, k_ref, v_ref, *, q_scale):
    u = _rms(h_ref[...], g_ref[...]).astype(BF16)
    n = q_ref.shape[1]
    for c in range(n // 256):
        sl = slice(c * 256, (c + 1) * 256)
        q_ref[:, sl] = (_dot(u, w_ref[:, c * 256:(c + 1) * 256]) * q_scale).astype(BF16)
        k_ref[:, sl] = _dot(u, w_ref[:, n + c * 256:n + (c + 1) * 256]).astype(BF16)
        v_ref[:, sl] = _dot(u, w_ref[:, 2 * n + c * 256:2 * n + (c + 1) * 256]).astype(BF16)


def _diff_prep(h, g, w, j):
    rows = h.shape[0]
    tm = PREP_TILE if rows % PREP_TILE == 0 else ROW_TILE
    n = w.shape[2] // 3
    row = lambda i: (i, 0)
    return pl.pallas_call(
        functools.partial(_diff_prep_kernel, q_scale=DIFF_HEAD_DIM ** -0.5 * LOG2E),
        grid=(rows // tm,),
        in_specs=[pl.BlockSpec((tm, D_MODEL), row), _full(g.shape), _layer(w, j)],
        out_specs=[pl.BlockSpec((tm, n), row)] * 3,
        out_shape=[jax.ShapeDtypeStruct((rows, n), BF16)] * 3,
        compiler_params=_cparams(1),
        name="diff_prep",
    )(h, g, w)


BIAS_SPLIT = 32


def _split3(x):
    c1 = x.astype(BF16).astype(F32)
    c2 = (x - c1).astype(BF16).astype(F32)
    c3 = (x - c1 - c2).astype(BF16).astype(F32)
    return c1, c2, c3


def _diff_attn_kernel(slope_ref, q_ref, k_ref, v_ref, lam_ref, g_ref, o_ref, *,
                      seq, tq, tk, lambda_init):
    slope2 = slope_ref[pl.program_id(1)] * LOG2E
    lp4 = lam_ref[...]
    lam = (jnp.exp(jnp.sum(lp4[0:1] * lp4[1:2], axis=1, keepdims=True))
           - jnp.exp(jnp.sum(lp4[2:3] * lp4[3:4], axis=1, keepdims=True)) + lambda_init)
    g = g_ref[...]
    dv = v_ref.shape[1]
    n_chunks = seq // tk
    per_tile = tq // tk

    def lanes3(lane, x1, x2, x3):
        return jnp.where(lane % 3 == 0, x1, jnp.where(lane % 3 == 1, x2, x3))

    c1, c2, c3 = _split3(jnp.full((1, 128), slope2, F32))
    lane_k = lax.broadcasted_iota(jnp.int32, (tk, 128), 1)
    jj = lax.broadcasted_iota(jnp.int32, (tk, 128), 0).astype(F32)
    piece = lanes3(lane_k, c1, c2, c3)
    k_const = jnp.where(lane_k < 3, jj,
                        jnp.where(lane_k < 6, float(BIAS_SPLIT) * piece,
                                  jnp.where(lane_k < 9, piece, 0.0))).astype(BF16)

    def k_aug(k0, n, const):
        return jnp.concatenate([k_ref[pl.ds(k0, n), :], const], axis=1)

    def bias_rows(rows, first):
        ii = lax.broadcasted_iota(jnp.int32, (rows, 128), 0)
        if first:
            ii = ii - N_META
        lane_q = lax.broadcasted_iota(jnp.int32, (rows, 128), 1)
        in_a = (lane_q >= 3) & (lane_q < 6)
        base = jnp.where(lane_q < 3, lanes3(lane_q, c1, c2, c3),
                         jnp.where(in_a, -(ii >> 5).astype(F32),
                                   jnp.where(lane_q < 9, -(ii & (BIAS_SPLIT - 1)).astype(F32), 0.0))
                         ).astype(BF16)
        return base, jnp.where(in_a, -1.0, 0.0).astype(BF16)

    def above_diag(rows, width, off, first):
        d = (lax.broadcasted_iota(jnp.int32, (rows, width), 1)
             - lax.broadcasted_iota(jnp.int32, (rows, width), 0))
        if first:
            d = d + N_META
        return jnp.maximum((d - off).astype(F32), 0.0) * (2.0 * slope2)

    def q_tile(r0, rows, t, base, unit, fixes):
        first = isinstance(t, int)
        q = q_ref[pl.ds(r0, rows), :]
        lane = lax.broadcasted_iota(jnp.int32, q.shape, 1)
        zero = jnp.zeros_like(q)
        qs = (jnp.where(lane < DIFF_HEAD_DIM, q, zero), jnp.where(lane >= DIFF_HEAD_DIM, q, zero))

        def q_aug(sgn, n_hi):
            tt = base + unit * (n_hi if isinstance(n_hi, float) else n_hi.astype(BF16))
            if not isinstance(sgn, float):
                tt = tt * sgn.astype(BF16)
            elif sgn != 1.0:
                tt = tt * sgn
            tt = tt.astype(BF16)
            return [jnp.concatenate([qs[c], tt], axis=1) for c in range(2)]

        base_row = ROW_OFF + t * tq
        n_meta = float(base_row // BIAS_SPLIT) if first else (base_row // BIAS_SPLIT).astype(F32)
        qa = q_aug(1.0, n_meta)
        kk = k_aug(0, ROW_OFF, k_const[0:ROW_OFF])
        col = lax.broadcasted_iota(jnp.int32, (rows, ROW_OFF), 1)
        vv = _v_ones(v_ref, 0, ROW_OFF)
        fix = above_diag(rows, ROW_OFF, ROW_OFF, True) if first else None
        m, acc = [], []
        for c in range(2):
            s = _dot_nt(qa[c], kk)
            if first:
                s = s - fix
            s = jnp.where(col >= N_PAD, s, NEG_BIG)
            mc = jnp.max(s, axis=1, keepdims=True)
            m.append(mc)
            acc.append(_dot(jnp.exp2(s - mc).astype(BF16), vv))

        for rel in range(n_chunks):
            diag = rel < per_tile
            if first:
                sgn = 1.0 if diag else -1.0
                n_hi = float(-(rel * tk) // BIAS_SPLIT)
                k0 = ROW_OFF + rel * tk
            else:
                raw = t * per_tile + rel
                wrapped = raw >= n_chunks
                c_idx = jnp.where(wrapped, raw - n_chunks, raw)
                sgn = 1.0 if diag else jnp.where(wrapped, 1.0, -1.0)
                n_hi = ((t * per_tile - c_idx) * (tk // BIAS_SPLIT)).astype(F32)
                k0 = pl.multiple_of(ROW_OFF + c_idx * tk, 128)
            qa = q_aug(sgn, n_hi)
            kk = k_aug(k0, tk, k_const)
            vv = _v_ones(v_ref, k0, tk)
            fix = fixes[rel] if diag else None
            for c in range(2):
                s = _dot_nt(qa[c], kk)
                if diag:
                    s = s - fix
                m_new = jnp.maximum(m[c], jnp.max(s, axis=1, keepdims=True))
                p = jnp.exp2(s - m_new)
                acc[c] = jnp.exp2(m[c] - m_new) * acc[c] + _dot(p.astype(BF16), vv)
                m[c] = m_new
        o = acc[0][:, :dv] / acc[0][:, dv:] - lam * (acc[1][:, :dv] / acc[1][:, dv:])
        o = _rms(o, g) * (1.0 - lambda_init)
        o_ref[pl.ds(r0, rows), :] = o.astype(BF16)

    o_ref[0:N_PAD, :] = jnp.zeros((N_PAD, o_ref.shape[1]), o_ref.dtype)
    rows0 = N_META + tq
    q_tile(N_PAD, rows0, 0, *bias_rows(rows0, True),
           [above_diag(rows0, tk, -rel * tk, True) for rel in range(per_tile)])

    base, unit = bias_rows(tq, False)
    fixes = [above_diag(tq, tk, -rel * tk, False) for rel in range(per_tile)]

    def body(i, carry):
        q_tile(pl.multiple_of(ROW_OFF + i * tq, 128), tq, i, base, unit, fixes)
        return carry

    lax.fori_loop(1, seq // tq, body, 0, unroll=2)


def _diff_attn(slopes, q, k, v, lam_p, g, seq, lambda_init):
    b, lp, _ = q.shape
    tq = _pick_tile(seq, 512)
    tk = _pick_tile(seq, 256)
    assert tk % BIAS_SPLIT == 0 and tk <= 256 and (lp + tq) // BIAS_SPLIT < 256
    w = 2 * DIFF_HEAD_DIM
    blk = pl.BlockSpec((None, lp, w), lambda i, j: (i, 0, j))
    return pl.pallas_call(
        functools.partial(_diff_attn_kernel, seq=seq, tq=tq, tk=tk, lambda_init=lambda_init),
        grid=(b, DIFF_HEADS),
        in_specs=[pl.BlockSpec(memory_space=pltpu.SMEM), blk, blk, blk,
                  _full(lam_p.shape), _full(g.shape)],
        out_specs=blk,
        out_shape=jax.ShapeDtypeStruct((b, lp, DIFF_HEADS * w), BF16),
        compiler_params=_cparams(2),
        name="diff_attn",
    )(slopes, q, k, v, lam_p, g)


def _gelu_tanh(x):
    c = math.sqrt(2.0 / math.pi)
    return x * (0.5 * (1.0 + jnp.tanh(c * (x + 0.044715 * (x * x * x)))))


def _lru_prep_kernel(h_ref, g_ref, w_ref, xb_ref, gate_ref):
    u = _rms(h_ref[...], g_ref[...]).astype(BF16)
    for c in range(LRU_WIDTH // 256):
        sl = slice(c * 256, (c + 1) * 256)
        xb_ref[:, sl] = _dot(u, w_ref[:, c * 256:(c + 1) * 256])
        gate = _dot(u, w_ref[:, LRU_WIDTH + c * 256:LRU_WIDTH + (c + 1) * 256])
        gate_ref[:, sl] = _gelu_tanh(gate).astype(BF16)


def _lru_prep(h, g, w, j):
    rows = h.shape[0]
    tm = PREP_TILE if rows % PREP_TILE == 0 else ROW_TILE
    row = lambda i: (i, 0)
    return pl.pallas_call(
        _lru_prep_kernel,
        grid=(rows // tm,),
        in_specs=[pl.BlockSpec((tm, D_MODEL), row), _full(g.shape), _layer(w, j)],
        out_specs=[pl.BlockSpec((tm, LRU_WIDTH), row)] * 2,
        out_shape=[jax.ShapeDtypeStruct((rows, LRU_WIDTH), F32),
                   jax.ShapeDtypeStruct((rows, LRU_WIDTH), BF16)],
        compiler_params=_cparams(1),
        name="lru_prep",
    )(h, g, w)


def _softplus(x):
    return jnp.maximum(x, 0.0) + jnp.log(1.0 + jnp.exp(-jnp.abs(x)))


def _group_scan(a, u, carry, reverse):
    row = lax.broadcasted_iota(jnp.int32, a.shape, 0)
    for k in (1, 2, 4):
        shift = (8 - k) if reverse else k
        keep = (row < 8 - k) if reverse else (row >= k)
        a_s = pltpu.roll(a, shift, 0)
        u_s = pltpu.roll(u, shift, 0)
        u = jnp.where(keep, a * u_s + u, u)
        a = jnp.where(keep, a * a_s, a)
    return a * carry + u


def _odd_part(n):
    while n % 2 == 0:
        n //= 2
    return n


def _lru_scan_kernel(xa_ref, xh_ref, gate_ref, cw_ref, cb_ref, wg_ref, bg_ref, lam_ref, o_ref,
                     xc_ref, y_ref, yb_ref, *, lp, seg):
    ch = LRU_CHUNK
    width = 2 * HALF
    sc = 8 * seg
    n_chunks = lp // sc
    x_refs = (xa_ref, xh_ref)

    def strided(t0, j):
        return pl.ds(t0 + j, 8, stride=seg)

    row8 = lax.broadcasted_iota(jnp.int32, (8, HALF), 0)
    for c in range(n_chunks):
        t0 = c * sc
        for hh in range(2):
            lanes = slice(hh * HALF, (hh + 1) * HALF)
            cw = cw_ref[:, lanes]
            cb = cb_ref[:, lanes]
            xs = {j: x_refs[hh][strided(t0, j), :] for j in range(-2 if c else 0, seg + (1 if c < n_chunks - 1 else 0))}
            if c == 0:
                for j in (-2, -1):
                    xs[j] = jnp.where(row8 >= 1, pltpu.roll(xs[seg + j], 1, 0), 0.0)
            if c == n_chunks - 1:
                xs[seg] = jnp.where(row8 <= 6, pltpu.roll(xs[0], 7, 0), 0.0)
            for j in range(seg):
                acc = cb + cw[0:1] * xs[j - 2] + cw[1:2] * xs[j - 1] + cw[2:3] * xs[j] + cw[3:4] * xs[j + 1]
                if t0 + j < N_PAD:
                    acc = jnp.where(row8 * seg + (t0 + j) >= N_PAD, acc, 0.0)
                xc_ref[t0 + 8 * j:t0 + 8 * j + 8, lanes] = acc

    def gates(xc, d, rate):
        xcb = xc.astype(BF16)
        r = jax.nn.sigmoid(_dot(xcb, wg_ref[d, 0]) + bg_ref[2 * d:2 * d + 1, :])
        i = jax.nn.sigmoid(_dot(xcb, wg_ref[d, 1]) + bg_ref[2 * d + 1:2 * d + 2, :])
        a = jnp.exp2(r * rate)
        x = 1.0 - a * a
        u = jnp.where(x > 0.0, x * lax.rsqrt(x), 0.0) * (i * xc)
        return a, u

    def scan_chunk(a, u, carry, reverse):
        hs, ps = [None] * seg, [None] * seg
        h = p = None
        for j in (reversed(range(seg)) if reverse else range(seg)):
            aj, uj = a[8 * j:8 * j + 8], u[8 * j:8 * j + 8]
            h = uj if h is None else aj * h + uj
            p = aj if p is None else aj * p
            hs[j], ps[j] = h, p
        ends = _group_scan(p, h, carry, reverse)
        row = lax.broadcasted_iota(jnp.int32, ends.shape, 0)
        if reverse:
            entering = jnp.where(row == 7, carry, pltpu.roll(ends, 7, 0))
            carry = ends[0:1]
        else:
            entering = jnp.where(row == 0, carry, pltpu.roll(ends, 1, 0))
            carry = ends[7:8]
        return [hs[j] + ps[j] * entering for j in range(seg)], carry

    def rate(d):
        return _softplus(-lam_ref[d:d + 1, :]) * (-LRU_C * LOG2E)

    rates = (rate(0), rate(1))

    def put_perm(ref, t0, ys):
        for j in range(seg):
            ref[0, strided(t0, j), :] = ys[j][:, :HALF]
            ref[1, strided(t0, j), :] = ys[j][:, HALF:]

    def both(c, carries):
        cf, cbk = carries
        tf = pl.multiple_of(c * sc, 8)
        tb = pl.multiple_of((n_chunks - 1 - c) * sc, 8)
        af, uf = gates(xc_ref[pl.ds(tf, sc), :], 0, rates[0])
        ab, ub = gates(xc_ref[pl.ds(tb, sc), :], 1, rates[1])
        yf, cf = scan_chunk(af, uf, cf, False)
        yb, cbk = scan_chunk(ab, ub, cbk, True)
        put_perm(y_ref, tf, yf)
        put_perm(yb_ref, tb, yb)
        return cf, cbk

    zero = jnp.zeros((1, width), F32)
    lax.fori_loop(0, n_chunks, both, (zero, zero), unroll=LRU_UNROLL)

    def finish(c, carry):
        rows = pl.ds(pl.multiple_of(c * ch, 128), ch)
        y = jnp.concatenate([y_ref[0, rows, :] + yb_ref[0, rows, :],
                             y_ref[1, rows, :] + yb_ref[1, rows, :]], axis=1)
        o_ref[rows, :] = (y * gate_ref[rows, :].astype(F32)).astype(BF16)
        return carry

    lax.fori_loop(0, lp // ch, finish, 0)


def _lru_scan(xb, gate, cw, cb, wg, bg, lam):
    b, lp, _ = xb.shape
    w = LRU_BLOCK_W
    seg = _odd_part(lp // 8)
    blk = pl.BlockSpec((None, lp, w), lambda i, j: (i, 0, j))
    half = lambda k: pl.BlockSpec((None, lp, HALF), lambda i, j: (i, 0, 2 * j + k))
    col = lambda r: pl.BlockSpec((r, w), lambda i, j: (0, j))
    return pl.pallas_call(
        functools.partial(_lru_scan_kernel, lp=lp, seg=seg),
        grid=(b, LRU_BLOCKS),
        in_specs=[half(0), half(1), blk, col(4), col(1),
                  pl.BlockSpec((2, 2, None, w, w), lambda i, j: (0, 0, j, 0, 0)),
                  col(4), col(2)],
        out_specs=blk,
        out_shape=jax.ShapeDtypeStruct((b, lp, LRU_WIDTH), BF16),
        scratch_shapes=[pltpu.VMEM((lp, w), F32), pltpu.VMEM((2, lp, HALF), F32), pltpu.VMEM((2, lp, HALF), F32)],
        compiler_params=_cparams(2),
        name="lru_scan",
    )(xb, xb, gate, cw, cb, wg, bg, lam)


def _load_cast(src, dst, chunk):
    rows, cols = dst.shape
    n = rows // chunk

    def body(stage, sem):
        def dma(k):
            return pltpu.make_async_copy(src.at[pl.ds(k * chunk, chunk), :], stage.at[k % 2], sem.at[k % 2])

        dma(0).start()
        for k in range(n):
            if k + 1 < n:
                dma(k + 1).start()
            dma(k).wait()
            dst[k * chunk:(k + 1) * chunk, :] = stage[k % 2].astype(BF16)

    pl.run_scoped(body, pltpu.VMEM((2, chunk, cols), F32), pltpu.SemaphoreType.DMA((2,)))


def _post_kernel(o_ref, h_ref, wo_hbm, g_ref, win_hbm, wout_hbm, out_ref, wo_ref, win_ref, wout_ref, *,
                 j, i, n_axes):
    first = pl.program_id(0) == 0
    for ax in range(1, n_axes):
        first = jnp.logical_and(first, pl.program_id(ax) == 0)

    @pl.when(first)
    def _():
        _load_cast(wo_hbm.at[j], wo_ref, 256)
        _load_cast(win_hbm.at[i], win_ref, 64)
        _load_cast(wout_hbm.at[i], wout_ref, 256)

    g = g_ref[...]
    x = h_ref[...] + _rms(_dot(o_ref[...], wo_ref[...]), g[1:2])
    xn = _rms(x, g[2:3]).astype(BF16)
    gg = _dot(xn, win_ref[:, :FFN_HIDDEN])
    uu = _dot(xn, win_ref[:, FFN_HIDDEN:])
    a = (gg * jax.nn.sigmoid(gg) * uu).astype(BF16)
    out_ref[...] = x + _rms(_dot(a, wout_ref[...]), g[3:4])


def _post(o, h, w_o, j, g, w_in, w_out, i, final=None):
    rows = h.shape[0]
    hbm = pl.BlockSpec(memory_space=pl.ANY)
    weights = [hbm, _full(g.shape), hbm, hbm]
    if final is None:
        tm = POST_TILE if rows % POST_TILE == 0 else ROW_TILE
        grid = (rows // tm,)
        row = lambda r: (r, 0)
        acts = [pl.BlockSpec((tm, o.shape[1]), row), pl.BlockSpec((tm, D_MODEL), row)]
        out_rows, out_spec = rows, pl.BlockSpec((tm, D_MODEL), row)
    else:
        b, lp, seq = final
        tm = _pick_tile(seq, POST_TILE)
        per = seq // tm
        grid = (b, per)
        start = lambda bi, t: (pl.multiple_of(bi * lp + ROW_OFF + t * tm, 128), 0)
        acts = [pl.BlockSpec((pl.Element(tm), pl.Element(o.shape[1])), start),
                pl.BlockSpec((pl.Element(tm), pl.Element(D_MODEL)), start)]
        out_rows, out_spec = b * seq, pl.BlockSpec((tm, D_MODEL), lambda bi, t: (bi * per + t, 0))
    return pl.pallas_call(
        functools.partial(_post_kernel, j=j, i=i, n_axes=len(grid)),
        grid=grid,
        in_specs=acts + weights,
        out_specs=out_spec,
        out_shape=jax.ShapeDtypeStruct((out_rows, D_MODEL), F32),
        scratch_shapes=[pltpu.VMEM(w_o.shape[1:], BF16), pltpu.VMEM(w_in.shape[1:], BF16),
                        pltpu.VMEM(w_out.shape[1:], BF16)],
        compiler_params=_cparams(len(grid)),
        name="post_ffn",
    )(o, h, w_o, g, w_in, w_out)


def _rope_rows(lp):
    pos = jnp.arange(lp, dtype=F32) - float(N_PAD)
    inv_freq = ROPE_THETA ** (-jnp.arange(0, MLA_ROPE, 2, dtype=F32) / MLA_ROPE)
    ang = pos[:, None] * inv_freq[None, :]
    c, s = jnp.cos(ang), jnp.sin(ang)
    z64 = jnp.zeros((lp, 64), F32)
    return jnp.concatenate([c, c, z64], axis=1), jnp.concatenate([s, s, z64], axis=1)


def _with_swapped(rope):
    half = rope.shape[-1] // 2
    return jnp.concatenate([rope, -rope[..., half:], rope[..., :half]], axis=-1)


def kernel(x, meta_tokens, norm_g, mla_w_in, mla_q_norm, mla_kv_norm, mla_w_uq, mla_w_ukv, mla_w_o,
           diff_w_in, diff_lambda, diff_subln, diff_w_o, lru_w_in, lru_conv_w, lru_conv_b,
           lru_w_gates, lru_b_gates, lru_lambda, lru_w_o, ffn_w_in, ffn_w_out):
    b, seq, d = x.shape
    depth = norm_g.shape[0]
    lp = seq + ROW_OFF
    assert d == D_MODEL and seq % 128 == 0 and lp % LRU_CHUNK == 0 and lp % ROW_TILE == 0

    meta = jnp.broadcast_to(meta_tokens[None].astype(x.dtype), (b, N_META, d))
    h = jnp.concatenate([jnp.zeros((b, N_PAD, d), x.dtype), meta, x], axis=1).reshape(b * lp, d)
    cos, sin = _rope_rows(lp)
    slopes = 2.0 ** (-8.0 * jnp.arange(1, DIFF_HEADS + 1, dtype=F32) / DIFF_HEADS)

    n_mla = mla_w_in.shape[0]
    lat = MLA_Q_RANK + MLA_KV_RANK
    m_in = jnp.concatenate([mla_w_in[..., :lat], _with_swapped(mla_w_in[..., lat:])], axis=-1).astype(BF16)
    uq = mla_w_uq.reshape(n_mla, MLA_Q_RANK, MLA_HEADS, MLA_NOPE + MLA_ROPE)
    uq = jnp.concatenate([uq[..., :MLA_NOPE], _with_swapped(uq[..., MLA_NOPE:])], axis=-1)
    m_uq = uq.reshape(n_mla, MLA_Q_RANK, MLA_HEADS * MLA_QK).astype(BF16)
    ukv = mla_w_ukv.reshape(n_mla, MLA_KV_RANK, MLA_HEADS, MLA_NOPE + MLA_V)
    m_uk = ukv[..., :MLA_NOPE].reshape(n_mla, MLA_KV_RANK, MLA_HEADS * MLA_NOPE).astype(BF16)
    m_uv = ukv[..., MLA_NOPE:].reshape(n_mla, MLA_KV_RANK, MLA_HEADS * MLA_V).astype(BF16)
    m_qg, m_kvg = mla_q_norm[:, None, :], mla_kv_norm[:, None, :]
    w_o_stacks = (mla_w_o, diff_w_o, lru_w_o)
    d_in, l_in = diff_w_in.astype(BF16), lru_w_in.astype(BF16)

    for i in range(depth):
        kind, j = i % 3, i // 3
        g = norm_g[i]
        if kind == 0:
            q, kn, kr, v = _mla_prep(h, g[0:1], j, m_in, m_qg, m_kvg, m_uq, m_uk, m_uv, cos, sin, lp)
            o = _mla_attn(q.reshape(b, lp, -1), kn.reshape(b, lp, -1), kr.reshape(b, lp, -1),
                          v.reshape(b, lp, -1), seq)
        elif kind == 1:
            lambda_init = 0.8 - 0.6 * math.exp(-0.3 * i)
            q, k, v = _diff_prep(h, g[0:1], d_in, j)
            o = _diff_attn(slopes, q.reshape(b, lp, -1), k.reshape(b, lp, -1), v.reshape(b, lp, -1),
                           diff_lambda[j], diff_subln[j][None], seq, lambda_init)
        else:
            xb, gate = _lru_prep(h, g[0:1], l_in, j)
            o = _lru_scan(xb.reshape(b, lp, -1), gate.reshape(b, lp, -1), lru_conv_w[j],
                          lru_conv_b[j][None], lru_w_gates[j].astype(BF16),
                          lru_b_gates[j].reshape(4, LRU_WIDTH), lru_lambda[j])
        last = (b, lp, seq) if i == depth - 1 else None
        h = _post(o.reshape(b * lp, -1), h, w_o_stacks[kind], j, g, ffn_w_in, ffn_w_out, i, final=last)
    return h.reshape(b, seq, d)
```

```python
import functools
import math

import jax
import jax.numpy as jnp
from jax import lax
from jax.experimental import pallas as pl
from jax.experimental.pallas import tpu as pltpu

F32 = jnp.float32
BF16 = jnp.bfloat16

D_MODEL = 1024
N_META = 16
ROW_OFF = 128
N_PAD = ROW_OFF - N_META
NORM_EPS = 1e-6
LOG2E = math.log2(math.e)
NEG_BIG = -1e30

MLA_HEADS = 16
MLA_Q_RANK = 384
MLA_KV_RANK = 256
MLA_NOPE = 128
MLA_ROPE = 64
MLA_V = 128
MLA_QK = 256
ROPE_THETA = 10000.0

DIFF_HEADS = 8
DIFF_HEAD_DIM = 64

LRU_WIDTH = 1536
LRU_BLOCKS = 6
LRU_BLOCK_W = 256
LRU_C = 8.0
LRU_CHUNK = 384
HALF = 128
LRU_UNROLL = 8

FFN_HIDDEN = 2816

ROW_TILE = 384
PREP_TILE = 768
POST_TILE = 512
VMEM_LIMIT = 56 * 1024 * 1024


def _cparams(n_axes):
    return pltpu.CompilerParams(dimension_semantics=("arbitrary",) * n_axes,
                                vmem_limit_bytes=VMEM_LIMIT)


def _rms(x, g):
    return x * lax.rsqrt(jnp.mean(x * x, axis=-1, keepdims=True) + NORM_EPS) * g


def _dot(a, b):
    return jnp.dot(a, b, preferred_element_type=F32)


def _dot_nt(a, b):
    return lax.dot_general(a, b, (((1,), (1,)), ((), ())), preferred_element_type=F32)


def _full(shape):
    nd = len(shape)
    return pl.BlockSpec(shape, lambda *_: (0,) * nd)


def _layer(a, i, **kw):
    nd = a.ndim - 1
    return pl.BlockSpec((None,) + a.shape[1:], lambda *_: (i,) + (0,) * nd, **kw)


def _rope128(v, cos, sin):
    return v * cos + pltpu.roll(v, 64, 1) * sin


def _mla_prep_kernel(h_ref, g_ref, win_ref, qg_ref, kvg_ref, wuq_ref, wuk_ref, wuv_ref,
                     cos_ref, sin_ref, q_ref, kn_ref, kr_ref, v_ref, *, q_scale):
    u = _rms(h_ref[...], g_ref[...]).astype(BF16)
    t = _dot(u, win_ref[...])
    cq = _rms(t[:, :MLA_Q_RANK], qg_ref[...]).astype(BF16)
    ckv = _rms(t[:, MLA_Q_RANK:MLA_Q_RANK + MLA_KV_RANK], kvg_ref[...]).astype(BF16)
    cos, sin = cos_ref[...], sin_ref[...]
    kr_ref[...] = _rope128(t[:, MLA_Q_RANK + MLA_KV_RANK:], cos, sin).astype(BF16)
    for hp in range(MLA_HEADS // 2):
        sl = slice(hp * 256, (hp + 1) * 256)
        kn_ref[:, sl] = _dot(ckv, wuk_ref[:, sl]).astype(BF16)
        v_ref[:, sl] = _dot(ckv, wuv_ref[:, sl]).astype(BF16)
        for j in range(2):
            h = 2 * hp + j
            q = _dot(cq, wuq_ref[:, h * MLA_QK:(h + 1) * MLA_QK])
            q_ref[:, h * MLA_QK:h * MLA_QK + 128] = (q[:, :128] * q_scale).astype(BF16)
            q_ref[:, h * MLA_QK + 128:(h + 1) * MLA_QK] = (_rope128(q[:, 128:], cos, sin) * q_scale).astype(BF16)


def _mla_prep(h, g, j, w_in, qg, kvg, wuq, wuk, wuv, cos, sin, lp):
    rows = h.shape[0]
    tm = lp // 4 if lp % 64 == 0 else ROW_TILE
    tiles_per_seq = lp // tm
    row = lambda i: (i, 0)
    pos = lambda i: (i % tiles_per_seq, 0)
    q_scale = (MLA_NOPE + MLA_ROPE) ** -0.5 * LOG2E
    widths = (MLA_HEADS * MLA_QK, MLA_HEADS * MLA_NOPE, 128, MLA_HEADS * MLA_V)
    return pl.pallas_call(
        functools.partial(_mla_prep_kernel, q_scale=q_scale),
        grid=(rows // tm,),
        in_specs=[pl.BlockSpec((tm, D_MODEL), row), _full(g.shape), _layer(w_in, j),
                  _layer(qg, j), _layer(kvg, j), _layer(wuq, j), _layer(wuk, j),
                  _layer(wuv, j), pl.BlockSpec((tm, 128), pos), pl.BlockSpec((tm, 128), pos)],
        out_specs=[pl.BlockSpec((tm, n), row) for n in widths],
        out_shape=[jax.ShapeDtypeStruct((rows, n), BF16) for n in widths],
        compiler_params=_cparams(1),
        name="mla_prep",
    )(h, g, w_in, qg, kvg, wuq, wuk, wuv, cos, sin)


def _v_ones(v_ref, k0, n):
    return jnp.concatenate([v_ref[pl.ds(k0, n), :], jnp.ones((n, v_ref.shape[1]), BF16)], axis=1)


def _query_tiles(q_tile, o_ref, seq, tq):
    o_ref[0:N_PAD, :] = jnp.zeros((N_PAD, o_ref.shape[1]), o_ref.dtype)
    q_tile(N_PAD, N_META + tq)

    def body(i, carry):
        q_tile(pl.multiple_of(ROW_OFF + i * tq, 128), tq)
        return carry

    lax.fori_loop(1, seq // tq, body, 0)


def _pick_tile(seq, largest):
    t = largest
    while seq % t:
        t //= 2
    return t


def _mla_attn_kernel(q_ref, kn_ref, kr_ref, v_ref, o_ref, *, seq, tq, tk):
    dv = v_ref.shape[1]

    def keys(k0, n):
        return jnp.concatenate([kn_ref[k0:k0 + n, :], kr_ref[k0:k0 + n, :]], axis=1)

    def q_tile(r0, rows):
        q = q_ref[pl.ds(r0, rows), :]
        s = _dot_nt(q, keys(0, ROW_OFF))
        col = lax.broadcasted_iota(jnp.int32, (rows, ROW_OFF), 1)
        s = jnp.where(col >= N_PAD, s, NEG_BIG)
        m = jnp.max(s, axis=1, keepdims=True)
        acc = _dot(jnp.exp2(s - m).astype(BF16), _v_ones(v_ref, 0, ROW_OFF))
        for c in range(seq // tk):
            k0 = ROW_OFF + c * tk
            s = _dot_nt(q, keys(k0, tk))
            m_new = jnp.maximum(m, jnp.max(s, axis=1, keepdims=True))
            p = jnp.exp2(s - m_new)
            acc = jnp.exp2(m - m_new) * acc + _dot(p.astype(BF16), _v_ones(v_ref, k0, tk))
            m = m_new
        o_ref[pl.ds(r0, rows), :] = (acc[:, :dv] / acc[:, dv:]).astype(BF16)

    _query_tiles(q_tile, o_ref, seq, tq)


def _mla_attn(q, kn, kr, v, seq):
    b, lp, _ = q.shape
    tq = _pick_tile(seq, 4096)
    tk = _pick_tile(seq, 256)
    head = lambda w: pl.BlockSpec((None, lp, w), lambda i, j: (i, 0, j))
    return pl.pallas_call(
        functools.partial(_mla_attn_kernel, seq=seq, tq=tq, tk=tk),
        grid=(b, MLA_HEADS),
        in_specs=[head(MLA_QK), head(MLA_NOPE), pl.BlockSpec((None, lp, 128), lambda i, j: (i, 0, 0)),
                  head(MLA_V)],
        out_specs=head(MLA_V),
        out_shape=jax.ShapeDtypeStruct((b, lp, MLA_HEADS * MLA_V), BF16),
        compiler_params=_cparams(2),
        name="mla_attn",
    )(q, kn, kr, v)


def _diff_prep_kernel(h_ref, g_ref, w_ref, q_ref, k_ref, v_ref, *, q_scale):
    u = _rms(h_ref[...], g_ref[...]).astype(BF16)
    n = q_ref.shape[1]
    for c in range(n // 256):
        sl = slice(c * 256, (c + 1) * 256)
        q_ref[:, sl] = (_dot(u, w_ref[:, c * 256:(c + 1) * 256]) * q_scale).astype(BF16)
        k_ref[:, sl] = _dot(u, w_ref[:, n + c * 256:n + (c + 1) * 256]).astype(BF16)
        v_ref[:, sl] = _dot(u, w_ref[:, 2 * n + c * 256:2 * n + (c + 1) * 256]).astype(BF16)


def _diff_prep(h, g, w, j):
    rows = h.shape[0]
    tm = PREP_TILE if rows % PREP_TILE == 0 else ROW_TILE
    n = w.shape[2] // 3
    row = lambda i: (i, 0)
    return pl.pallas_call(
        functools.partial(_diff_prep_kernel, q_scale=DIFF_HEAD_DIM ** -0.5 * LOG2E),
        grid=(rows // tm,),
        in_specs=[pl.BlockSpec((tm, D_MODEL), row), _full(g.shape), _layer(w, j)],
        out_specs=[pl.BlockSpec((tm, n), row)] * 3,
        out_shape=[jax.ShapeDtypeStruct((rows, n), BF16)] * 3,
        compiler_params=_cparams(1),
        name="diff_prep",
    )(h, g, w)


BIAS_SPLIT = 32


def _split3(x):
    c1 = x.astype(BF16).astype(F32)
    c2 = (x - c1).astype(BF16).astype(F32)
    c3 = (x - c1 - c2).astype(BF16).astype(F32)
    return c1, c2, c3


def _diff_attn_kernel(slope_ref, q_ref, k_ref, v_ref, lam_ref, g_ref, o_ref, *,
                      seq, tq, tk, lambda_init):
    slope2 = slope_ref[pl.program_id(1)] * LOG2E
    lp4 = lam_ref[...]
    lam = (jnp.exp(jnp.sum(lp4[0:1] * lp4[1:2], axis=1, keepdims=True))
           - jnp.exp(jnp.sum(lp4[2:3] * lp4[3:4], axis=1, keepdims=True)) + lambda_init)
    g = g_ref[...]
    dv = v_ref.shape[1]
    n_chunks = seq // tk
    per_tile = tq // tk

    def lanes3(lane, x1, x2, x3):
        return jnp.where(lane % 3 == 0, x1, jnp.where(lane % 3 == 1, x2, x3))

    c1, c2, c3 = _split3(jnp.full((1, 128), slope2, F32))
    lane_k = lax.broadcasted_iota(jnp.int32, (tk, 128), 1)
    jj = lax.broadcasted_iota(jnp.int32, (tk, 128), 0).astype(F32)
    piece = lanes3(lane_k, c1, c2, c3)
    k_const = jnp.where(lane_k < 3, jj,
                        jnp.where(lane_k < 6, float(BIAS_SPLIT) * piece,
                                  jnp.where(lane_k < 9, piece, 0.0))).astype(BF16)

    def k_aug(k0, n, const):
        return jnp.concatenate([k_ref[pl.ds(k0, n), :], const], axis=1)

    def bias_rows(rows, first):
        ii = lax.broadcasted_iota(jnp.int32, (rows, 128), 0)
        if first:
            ii = ii - N_META
        lane_q = lax.broadcasted_iota(jnp.int32, (rows, 128), 1)
        in_a = (lane_q >= 3) & (lane_q < 6)
        base = jnp.where(lane_q < 3, lanes3(lane_q, c1, c2, c3),
                         jnp.where(in_a, -(ii >> 5).astype(F32),
                                   jnp.where(lane_q < 9, -(ii & (BIAS_SPLIT - 1)).astype(F32), 0.0))
                         ).astype(BF16)
        return base, jnp.where(in_a, -1.0, 0.0).astype(BF16)

    def above_diag(rows, width, off, first):
        d = (lax.broadcasted_iota(jnp.int32, (rows, width), 1)
             - lax.broadcasted_iota(jnp.int32, (rows, width), 0))
        if first:
            d = d + N_META
        return jnp.maximum((d - off).astype(F32), 0.0) * (2.0 * slope2)

    def q_tile(r0, rows, t, base, unit, fixes):
        first = isinstance(t, int)
        q = q_ref[pl.ds(r0, rows), :]
        lane = lax.broadcasted_iota(jnp.int32, q.shape, 1)
        zero = jnp.zeros_like(q)
        qs = (jnp.where(lane < DIFF_HEAD_DIM, q, zero), jnp.where(lane >= DIFF_HEAD_DIM, q, zero))

        def q_aug(sgn, n_hi):
            tt = base + unit * (n_hi if isinstance(n_hi, float) else n_hi.astype(BF16))
            if not isinstance(sgn, float):
                tt = tt * sgn.astype(BF16)
            elif sgn != 1.0:
                tt = tt * sgn
            tt = tt.astype(BF16)
            return [jnp.concatenate([qs[c], tt], axis=1) for c in range(2)]

        base_row = ROW_OFF + t * tq
        n_meta = float(base_row // BIAS_SPLIT) if first else (base_row // BIAS_SPLIT).astype(F32)
        qa = q_aug(1.0, n_meta)
        kk = k_aug(0, ROW_OFF, k_const[0:ROW_OFF])
        col = lax.broadcasted_iota(jnp.int32, (rows, ROW_OFF), 1)
        vv = _v_ones(v_ref, 0, ROW_OFF)
        fix = above_diag(rows, ROW_OFF, ROW_OFF, True) if first else None
        m, acc = [], []
        for c in range(2):
            s = _dot_nt(qa[c], kk)
            if first:
                s = s - fix
            s = jnp.where(col >= N_PAD, s, NEG_BIG)
            mc = jnp.max(s, axis=1, keepdims=True)
            m.append(mc)
            acc.append(_dot(jnp.exp2(s - mc).astype(BF16), vv))

        for rel in range(n_chunks):
            diag = rel < per_tile
            if first:
                sgn = 1.0 if diag else -1.0
                n_hi = float(-(rel * tk) // BIAS_SPLIT)
                k0 = ROW_OFF + rel * tk
            else:
                raw = t * per_tile + rel
                wrapped = raw >= n_chunks
                c_idx = jnp.where(wrapped, raw - n_chunks, raw)
                sgn = 1.0 if diag else jnp.where(wrapped, 1.0, -1.0)
                n_hi = ((t * per_tile - c_idx) * (tk // BIAS_SPLIT)).astype(F32)
                k0 = pl.multiple_of(ROW_OFF + c_idx * tk, 128)
            qa = q_aug(sgn, n_hi)
            kk = k_aug(k0, tk, k_const)
            vv = _v_ones(v_ref, k0, tk)
            fix = fixes[rel] if diag else None
            for c in range(2):
                s = _dot_nt(qa[c], kk)
                if diag:
                    s = s - fix
                m_new = jnp.maximum(m[c], jnp.max(s, axis=1, keepdims=True))
                p = jnp.exp2(s - m_new)
                acc[c] = jnp.exp2(m[c] - m_new) * acc[c] + _dot(p.astype(BF16), vv)
                m[c] = m_new
        o = acc[0][:, :dv] / acc[0][:, dv:] - lam * (acc[1][:, :dv] / acc[1][:, dv:])
        o = _rms(o, g) * (1.0 - lambda_init)
        o_ref[pl.ds(r0, rows), :] = o.astype(BF16)

    o_ref[0:N_PAD, :] = jnp.zeros((N_PAD, o_ref.shape[1]), o_ref.dtype)
    rows0 = N_META + tq
    q_tile(N_PAD, rows0, 0, *bias_rows(rows0, True),
           [above_diag(rows0, tk, -rel * tk, True) for rel in range(per_tile)])

    base, unit = bias_rows(tq, False)
    fixes = [above_diag(tq, tk, -rel * tk, False) for rel in range(per_tile)]

    def body(i, carry):
        q_tile(pl.multiple_of(ROW_OFF + i * tq, 128), tq, i, base, unit, fixes)
        return carry

    lax.fori_loop(1, seq // tq, body, 0, unroll=2)


def _diff_attn(slopes, q, k, v, lam_p, g, seq, lambda_init):
    b, lp, _ = q.shape
    tq = _pick_tile(seq, 512)
    tk = _pick_tile(seq, 256)
    assert tk % BIAS_SPLIT == 0 and tk <= 256 and (lp + tq) // BIAS_SPLIT < 256
    w = 2 * DIFF_HEAD_DIM
    blk = pl.BlockSpec((None, lp, w), lambda i, j: (i, 0, j))
    return pl.pallas_call(
        functools.partial(_diff_attn_kernel, seq=seq, tq=tq, tk=tk, lambda_init=lambda_init),
        grid=(b, DIFF_HEADS),
        in_specs=[pl.BlockSpec(memory_space=pltpu.SMEM), blk, blk, blk,
                  _full(lam_p.shape), _full(g.shape)],
        out_specs=blk,
        out_shape=jax.ShapeDtypeStruct((b, lp, DIFF_HEADS * w), BF16),
        compiler_params=_cparams(2),
        name="diff_attn",
    )(slopes, q, k, v, lam_p, g)


def _gelu_tanh(x):
    c = math.sqrt(2.0 / math.pi)
    return x * (0.5 * (1.0 + jnp.tanh(c * (x + 0.044715 * (x * x * x)))))


def _lru_prep_kernel(h_ref, g_ref, w_ref, xb_ref, gate_ref):
    u = _rms(h_ref[...], g_ref[...]).astype(BF16)
    for c in range(LRU_WIDTH // 256):
        sl = slice(c * 256, (c + 1) * 256)
        xb_ref[:, sl] = _dot(u, w_ref[:, c * 256:(c + 1) * 256])
        gate = _dot(u, w_ref[:, LRU_WIDTH + c * 256:LRU_WIDTH + (c + 1) * 256])
        gate_ref[:, sl] = _gelu_tanh(gate).astype(BF16)


def _lru_prep(h, g, w, j):
    rows = h.shape[0]
    tm = PREP_TILE if rows % PREP_TILE == 0 else ROW_TILE
    row = lambda i: (i, 0)
    return pl.pallas_call(
        _lru_prep_kernel,
        grid=(rows // tm,),
        in_specs=[pl.BlockSpec((tm, D_MODEL), row), _full(g.shape), _layer(w, j)],
        out_specs=[pl.BlockSpec((tm, LRU_WIDTH), row)] * 2,
        out_shape=[jax.ShapeDtypeStruct((rows, LRU_WIDTH), F32),
                   jax.ShapeDtypeStruct((rows, LRU_WIDTH), BF16)],
        compiler_params=_cparams(1),
        name="lru_prep",
    )(h, g, w)


def _softplus(x):
    return jnp.maximum(x, 0.0) + jnp.log(1.0 + jnp.exp(-jnp.abs(x)))


def _group_scan(a, u, carry, reverse):
    row = lax.broadcasted_iota(jnp.int32, a.shape, 0)
    for k in (1, 2, 4):
        shift = (8 - k) if reverse else k
        keep = (row < 8 - k) if reverse else (row >= k)
        a_s = pltpu.roll(a, shift, 0)
        u_s = pltpu.roll(u, shift, 0)
        u = jnp.where(keep, a * u_s + u, u)
        a = jnp.where(keep, a * a_s, a)
    return a * carry + u


def _odd_part(n):
    while n % 2 == 0:
        n //= 2
    return n


def _lru_scan_kernel(xa_ref, xh_ref, gate_ref, cw_ref, cb_ref, wg_ref, bg_ref, lam_ref, o_ref,
                     xc_ref, y_ref, yb_ref, *, lp, seg):
    ch = LRU_CHUNK
    width = 2 * HALF
    sc = 8 * seg
    n_chunks = lp // sc
    x_refs = (xa_ref, xh_ref)

    def strided(t0, j):
        return pl.ds(t0 + j, 8, stride=seg)

    row8 = lax.broadcasted_iota(jnp.int32, (8, HALF), 0)
    for c in range(n_chunks):
        t0 = c * sc
        for hh in range(2):
            lanes = slice(hh * HALF, (hh + 1) * HALF)
            cw = cw_ref[:, lanes]
            cb = cb_ref[:, lanes]
            xs = {j: x_refs[hh][strided(t0, j), :] for j in range(-2 if c else 0, seg + (1 if c < n_chunks - 1 else 0))}
            if c == 0:
                for j in (-2, -1):
                    xs[j] = jnp.where(row8 >= 1, pltpu.roll(xs[seg + j], 1, 0), 0.0)
            if c == n_chunks - 1:
                xs[seg] = jnp.where(row8 <= 6, pltpu.roll(xs[0], 7, 0), 0.0)
            for j in range(seg):
                acc = cb + cw[0:1] * xs[j - 2] + cw[1:2] * xs[j - 1] + cw[2:3] * xs[j] + cw[3:4] * xs[j + 1]
                if t0 + j < N_PAD:
                    acc = jnp.where(row8 * seg + (t0 + j) >= N_PAD, acc, 0.0)
                xc_ref[t0 + 8 * j:t0 + 8 * j + 8, lanes] = acc

    def gates(xc, d, rate):
        xcb = xc.astype(BF16)
        r = jax.nn.sigmoid(_dot(xcb, wg_ref[d, 0]) + bg_ref[2 * d:2 * d + 1, :])
        i = jax.nn.sigmoid(_dot(xcb, wg_ref[d, 1]) + bg_ref[2 * d + 1:2 * d + 2, :])
        a = jnp.exp2(r * rate)
        x = 1.0 - a * a
        u = jnp.where(x > 0.0, x * lax.rsqrt(x), 0.0) * (i * xc)
        return a, u

    def scan_chunk(a, u, carry, reverse):
        hs, ps = [None] * seg, [None] * seg
        h = p = None
        for j in (reversed(range(seg)) if reverse else range(seg)):
            aj, uj = a[8 * j:8 * j + 8], u[8 * j:8 * j + 8]
            h = uj if h is None else aj * h + uj
            p = aj if p is None else aj * p
            hs[j], ps[j] = h, p
        ends = _group_scan(p, h, carry, reverse)
        row = lax.broadcasted_iota(jnp.int32, ends.shape, 0)
        if reverse:
            entering = jnp.where(row == 7, carry, pltpu.roll(ends, 7, 0))
            carry = ends[0:1]
        else:
            entering = jnp.where(row == 0, carry, pltpu.roll(ends, 1, 0))
            carry = ends[7:8]
        return [hs[j] + ps[j] * entering for j in range(seg)], carry

    def rate(d):
        return _softplus(-lam_ref[d:d + 1, :]) * (-LRU_C * LOG2E)

    rates = (rate(0), rate(1))

    def put_perm(ref, t0, ys):
        for j in range(seg):
            ref[0, strided(t0, j), :] = ys[j][:, :HALF]
            ref[1, strided(t0, j), :] = ys[j][:, HALF:]

    def both(c, carries):
        cf, cbk = carries
        tf = pl.multiple_of(c * sc, 8)
        tb = pl.multiple_of((n_chunks - 1 - c) * sc, 8)
        af, uf = gates(xc_ref[pl.ds(tf, sc), :], 0, rates[0])
        ab, ub = gates(xc_ref[pl.ds(tb, sc), :], 1, rates[1])
        yf, cf = scan_chunk(af, uf, cf, False)
        yb, cbk = scan_chunk(ab, ub, cbk, True)
        put_perm(y_ref, tf, yf)
        put_perm(yb_ref, tb, yb)
        return cf, cbk

    zero = jnp.zeros((1, width), F32)
    lax.fori_loop(0, n_chunks, both, (zero, zero), unroll=LRU_UNROLL)

    def finish(c, carry):
        rows = pl.ds(pl.multiple_of(c * ch, 128), ch)
        y = jnp.concatenate([y_ref[0, rows, :] + yb_ref[0, rows, :],
                             y_ref[1, rows, :] + yb_ref[1, rows, :]], axis=1)
        o_ref[rows, :] = (y * gate_ref[rows, :].astype(F32)).astype(BF16)
        return carry

    lax.fori_loop(0, lp // ch, finish, 0)


def _lru_scan(xb, gate, cw, cb, wg, bg, lam):
    b, lp, _ = xb.shape
    w = LRU_BLOCK_W
    seg = _odd_part(lp // 8)
    blk = pl.BlockSpec((None, lp, w), lambda i, j: (i, 0, j))
    half = lambda k: pl.BlockSpec((None, lp, HALF), lambda i, j: (i, 0, 2 * j + k))
    col = lambda r: pl.BlockSpec((r, w), lambda i, j: (0, j))
    return pl.pallas_call(
        functools.partial(_lru_scan_kernel, lp=lp, seg=seg),
        grid=(b, LRU_BLOCKS),
        in_specs=[half(0), half(1), blk, col(4), col(1),
                  pl.BlockSpec((2, 2, None, w, w), lambda i, j: (0, 0, j, 0, 0)),
                  col(4), col(2)],
        out_specs=blk,
        out_shape=jax.ShapeDtypeStruct((b, lp, LRU_WIDTH), BF16),
        scratch_shapes=[pltpu.VMEM((lp, w), F32), pltpu.VMEM((2, lp, HALF), F32), pltpu.VMEM((2, lp, HALF), F32)],
        compiler_params=_cparams(2),
        name="lru_scan",
    )(xb, xb, gate, cw, cb, wg, bg, lam)


def _post_kernel(o_ref, h_ref, wo_ref, g_ref, win_ref, wout_ref, out_ref):
    g = g_ref[...]
    x = h_ref[...] + _rms(_dot(o_ref[...], wo_ref[...]), g[1:2])
    xn = _rms(x, g[2:3]).astype(BF16)
    gg = _dot(xn, win_ref[:, :FFN_HIDDEN])
    uu = _dot(xn, win_ref[:, FFN_HIDDEN:])
    a = (gg * jax.nn.sigmoid(gg) * uu).astype(BF16)
    out_ref[...] = x + _rms(_dot(a, wout_ref[...]), g[3:4])


def _post(o, h, w_o, j, g, w_in, w_out, i, final=None):
    rows = h.shape[0]
    once = dict(pipeline_mode=pl.Buffered(1))
    weights = [_layer(w_o, j, **once), _full(g.shape), _layer(w_in, i, **once), _layer(w_out, i, **once)]
    if final is None:
        tm = POST_TILE if rows % POST_TILE == 0 else ROW_TILE
        grid = (rows // tm,)
        row = lambda r: (r, 0)
        acts = [pl.BlockSpec((tm, o.shape[1]), row), pl.BlockSpec((tm, D_MODEL), row)]
        out_rows, out_spec = rows, pl.BlockSpec((tm, D_MODEL), row)
    else:
        b, lp, seq = final
        tm = _pick_tile(seq, POST_TILE)
        per = seq // tm
        grid = (b, per)
        start = lambda bi, t: (pl.multiple_of(bi * lp + ROW_OFF + t * tm, 128), 0)
        acts = [pl.BlockSpec((pl.Element(tm), pl.Element(o.shape[1])), start),
                pl.BlockSpec((pl.Element(tm), pl.Element(D_MODEL)), start)]
        out_rows, out_spec = b * seq, pl.BlockSpec((tm, D_MODEL), lambda bi, t: (bi * per + t, 0))
    return pl.pallas_call(
        _post_kernel,
        grid=grid,
        in_specs=acts + weights,
        out_specs=out_spec,
        out_shape=jax.ShapeDtypeStruct((out_rows, D_MODEL), F32),
        compiler_params=_cparams(len(grid)),
        name="post_ffn",
    )(o, h, w_o, g, w_in, w_out)


def _rope_rows(lp):
    pos = jnp.arange(lp, dtype=F32) - float(N_PAD)
    inv_freq = ROPE_THETA ** (-jnp.arange(0, MLA_ROPE, 2, dtype=F32) / MLA_ROPE)
    ang = pos[:, None] * inv_freq[None, :]
    c, s = jnp.cos(ang), jnp.sin(ang)
    z64 = jnp.zeros((lp, 64), F32)
    return jnp.concatenate([c, c, z64], axis=1), jnp.concatenate([s, s, z64], axis=1)


def _with_swapped(rope):
    half = rope.shape[-1] // 2
    return jnp.concatenate([rope, -rope[..., half:], rope[..., :half]], axis=-1)


def kernel(x, meta_tokens, norm_g, mla_w_in, mla_q_norm, mla_kv_norm, mla_w_uq, mla_w_ukv, mla_w_o,
           diff_w_in, diff_lambda, diff_subln, diff_w_o, lru_w_in, lru_conv_w, lru_conv_b,
           lru_w_gates, lru_b_gates, lru_lambda, lru_w_o, ffn_w_in, ffn_w_out):
    b, seq, d = x.shape
    depth = norm_g.shape[0]
    lp = seq + ROW_OFF
    assert d == D_MODEL and seq % 128 == 0 and lp % LRU_CHUNK == 0 and lp % ROW_TILE == 0

    meta = jnp.broadcast_to(meta_tokens[None].astype(x.dtype), (b, N_META, d))
    h = jnp.concatenate([jnp.zeros((b, N_PAD, d), x.dtype), meta, x], axis=1).reshape(b * lp, d)
    cos, sin = _rope_rows(lp)
    slopes = 2.0 ** (-8.0 * jnp.arange(1, DIFF_HEADS + 1, dtype=F32) / DIFF_HEADS)

    n_mla = mla_w_in.shape[0]
    lat = MLA_Q_RANK + MLA_KV_RANK
    m_in = jnp.concatenate([mla_w_in[..., :lat], _with_swapped(mla_w_in[..., lat:])], axis=-1).astype(BF16)
    uq = mla_w_uq.reshape(n_mla, MLA_Q_RANK, MLA_HEADS, MLA_NOPE + MLA_ROPE)
    uq = jnp.concatenate([uq[..., :MLA_NOPE], _with_swapped(uq[..., MLA_NOPE:])], axis=-1)
    m_uq = uq.reshape(n_mla, MLA_Q_RANK, MLA_HEADS * MLA_QK).astype(BF16)
    ukv = mla_w_ukv.reshape(n_mla, MLA_KV_RANK, MLA_HEADS, MLA_NOPE + MLA_V)
    m_uk = ukv[..., :MLA_NOPE].reshape(n_mla, MLA_KV_RANK, MLA_HEADS * MLA_NOPE).astype(BF16)
    m_uv = ukv[..., MLA_NOPE:].reshape(n_mla, MLA_KV_RANK, MLA_HEADS * MLA_V).astype(BF16)
    m_qg, m_kvg = mla_q_norm[:, None, :], mla_kv_norm[:, None, :]
    w_o_stacks = (mla_w_o.astype(BF16), diff_w_o.astype(BF16), lru_w_o.astype(BF16))
    d_in, l_in = diff_w_in.astype(BF16), lru_w_in.astype(BF16)
    f_in, f_out = ffn_w_in.astype(BF16), ffn_w_out.astype(BF16)

    for i in range(depth):
        kind, j = i % 3, i // 3
        g = norm_g[i]
        if kind == 0:
            q, kn, kr, v = _mla_prep(h, g[0:1], j, m_in, m_qg, m_kvg, m_uq, m_uk, m_uv, cos, sin, lp)
            o = _mla_attn(q.reshape(b, lp, -1), kn.reshape(b, lp, -1), kr.reshape(b, lp, -1),
                          v.reshape(b, lp, -1), seq)
        elif kind == 1:
            lambda_init = 0.8 - 0.6 * math.exp(-0.3 * i)
            q, k, v = _diff_prep(h, g[0:1], d_in, j)
            o = _diff_attn(slopes, q.reshape(b, lp, -1), k.reshape(b, lp, -1), v.reshape(b, lp, -1),
                           diff_lambda[j], diff_subln[j][None], seq, lambda_init)
        else:
            xb, gate = _lru_prep(h, g[0:1], l_in, j)
            o = _lru_scan(xb.reshape(b, lp, -1), gate.reshape(b, lp, -1), lru_conv_w[j],
                          lru_conv_b[j][None], lru_w_gates[j].astype(BF16),
                          lru_b_gates[j].reshape(4, LRU_WIDTH), lru_lambda[j])
        last = (b, lp, seq) if i == depth - 1 else None
        h = _post(o.reshape(b * lp, -1), h, w_o_stacks[kind], j, g, f_in, f_out, i, final=last)
    return h.reshape(b, seq, d)
```

```python
import functools
import math

import jax
import jax.numpy as jnp
from jax import lax
from jax.experimental import pallas as pl
from jax.experimental.pallas import tpu as pltpu

F32 = jnp.float32
BF16 = jnp.bfloat16

D_MODEL = 1024
N_META = 16
ROW_OFF = 128
N_PAD = ROW_OFF - N_META
NORM_EPS = 1e-6
LOG2E = math.log2(math.e)
NEG_BIG = -1e30

MLA_HEADS = 16
MLA_Q_RANK = 384
MLA_KV_RANK = 256
MLA_NOPE = 128
MLA_ROPE = 64
MLA_V = 128
MLA_QK = 256
ROPE_THETA = 10000.0

DIFF_HEADS = 8
DIFF_HEAD_DIM = 64

LRU_WIDTH = 1536
LRU_BLOCKS = 6
LRU_BLOCK_W = 256
LRU_C = 8.0
LRU_CHUNK = 384
HALF = 128
LRU_UNROLL = 4

FFN_HIDDEN = 2816

ROW_TILE = 384
PREP_TILE = 768
POST_TILE = 512
VMEM_LIMIT = 56 * 1024 * 1024


def _cparams(n_axes):
    return pltpu.CompilerParams(dimension_semantics=("arbitrary",) * n_axes,
                                vmem_limit_bytes=VMEM_LIMIT)


def _rms(x, g):
    return x * lax.rsqrt(jnp.mean(x * x, axis=-1, keepdims=True) + NORM_EPS) * g


def _dot(a, b):
    return jnp.dot(a, b, preferred_element_type=F32)


def _dot_nt(a, b):
    return lax.dot_general(a, b, (((1,), (1,)), ((), ())), preferred_element_type=F32)


def _full(shape):
    nd = len(shape)
    return pl.BlockSpec(shape, lambda *_: (0,) * nd)


def _layer(a, i, **kw):
    nd = a.ndim - 1
    return pl.BlockSpec((None,) + a.shape[1:], lambda *_: (i,) + (0,) * nd, **kw)


def _rope128(v, cos, sin):
    return v * cos + pltpu.roll(v, 64, 1) * sin


def _mla_prep_body(hval, g_ref, win_ref, qg_ref, kvg_ref, wuq_ref, wuk_ref, wuv_ref,
                   cos_ref, sin_ref, q_ref, kn_ref, kr_ref, v_ref, q_scale):
    u = _rms(hval, g_ref[...]).astype(BF16)
    t = _dot(u, win_ref[...])
    cq = _rms(t[:, :MLA_Q_RANK], qg_ref[...]).astype(BF16)
    ckv = _rms(t[:, MLA_Q_RANK:MLA_Q_RANK + MLA_KV_RANK], kvg_ref[...]).astype(BF16)
    cos, sin = cos_ref[...], sin_ref[...]
    kr_ref[...] = _rope128(t[:, MLA_Q_RANK + MLA_KV_RANK:], cos, sin).astype(BF16)
    for hp in range(MLA_HEADS // 2):
        sl = slice(hp * 256, (hp + 1) * 256)
        kn_ref[:, sl] = _dot(ckv, wuk_ref[:, sl]).astype(BF16)
        v_ref[:, sl] = _dot(ckv, wuv_ref[:, sl]).astype(BF16)
        for j in range(2):
            h = 2 * hp + j
            q = _dot(cq, wuq_ref[:, h * MLA_QK:(h + 1) * MLA_QK])
            q_ref[:, h * MLA_QK:h * MLA_QK + 128] = (q[:, :128] * q_scale).astype(BF16)
            q_ref[:, h * MLA_QK + 128:(h + 1) * MLA_QK] = (_rope128(q[:, 128:], cos, sin) * q_scale).astype(BF16)


def _mla_prep_kernel(h_ref, *refs, q_scale):
    _mla_prep_body(h_ref[...], *refs, q_scale)


def _mla_prep_x_kernel(x_ref, head_ref, *refs, q_scale, tiles_per_seq):
    h_out = refs[-1]
    i = pl.program_id(0)
    blk = x_ref[...]
    n = blk.shape[0] - ROW_OFF
    body = jnp.where(i == 0, blk[0:n], blk[ROW_OFF:])
    first = jnp.concatenate([head_ref[...], body], axis=0)
    hval = jnp.where(i % tiles_per_seq == 0, first, blk)
    h_out[...] = hval
    _mla_prep_body(hval, *refs[:-1], q_scale)


def _mla_prep(h, g, j, w_in, qg, kvg, wuq, wuk, wuv, cos, sin, lp, x_head=None):
    rows = (x_head[0].shape[0] // x_head[2]) * lp if x_head else h.shape[0]
    if x_head:
        tm = lp // 8
    else:
        tm = lp // 4 if lp % 64 == 0 else ROW_TILE
    tiles_per_seq = lp // tm
    row = lambda i: (i, 0)
    pos = lambda i: (i % tiles_per_seq, 0)
    q_scale = (MLA_NOPE + MLA_ROPE) ** -0.5 * LOG2E
    widths = (MLA_HEADS * MLA_QK, MLA_HEADS * MLA_NOPE, 128, MLA_HEADS * MLA_V)
    common = [_full(g.shape), _layer(w_in, j), _layer(qg, j), _layer(kvg, j), _layer(wuq, j), _layer(wuk, j),
              _layer(wuv, j), pl.BlockSpec((tm, 128), pos), pl.BlockSpec((tm, 128), pos)]
    out_specs = [pl.BlockSpec((tm, n), row) for n in widths]
    out_shape = [jax.ShapeDtypeStruct((rows, n), BF16) for n in widths]
    if x_head is None:
        body, acts, args = functools.partial(_mla_prep_kernel, q_scale=q_scale), [pl.BlockSpec((tm, D_MODEL), row)], (h,)
    else:
        x2d, head, seq = x_head
        start = lambda i: (pl.multiple_of(jnp.maximum(
            (i // tiles_per_seq) * seq + (i % tiles_per_seq) * tm - ROW_OFF, 0), 8), 0)
        body = functools.partial(_mla_prep_x_kernel, q_scale=q_scale, tiles_per_seq=tiles_per_seq)
        acts = [pl.BlockSpec((pl.Element(tm), pl.Element(D_MODEL)), start), _full(head.shape)]
        args = (x2d, head)
        out_specs = out_specs + [pl.BlockSpec((tm, D_MODEL), row)]
        out_shape = out_shape + [jax.ShapeDtypeStruct((rows, D_MODEL), F32)]
    return pl.pallas_call(
        body,
        grid=(rows // tm,),
        in_specs=acts + common,
        out_specs=out_specs,
        out_shape=out_shape,
        compiler_params=_cparams(1),
        name="mla_prep",
    )(*args, g, w_in, qg, kvg, wuq, wuk, wuv, cos, sin)


def _v_ones(v_ref, k0, n):
    return jnp.concatenate([v_ref[pl.ds(k0, n), :], jnp.ones((n, v_ref.shape[1]), BF16)], axis=1)


def _query_tiles(q_tile, o_ref, seq, tq):
    o_ref[0:N_PAD, :] = jnp.zeros((N_PAD, o_ref.shape[1]), o_ref.dtype)
    q_tile(N_PAD, N_META + tq)

    def body(i, carry):
        q_tile(pl.multiple_of(ROW_OFF + i * tq, 128), tq)
        return carry

    lax.fori_loop(1, seq // tq, body, 0)


def _pick_tile(seq, largest):
    t = largest
    while seq % t:
        t //= 2
    return t


def _mla_attn_kernel(q_ref, kn_ref, kr_ref, v_ref, o_ref, *, seq, tq, tk):
    dv = v_ref.shape[1]

    def keys(k0, n):
        return jnp.concatenate([kn_ref[k0:k0 + n, :], kr_ref[k0:k0 + n, :]], axis=1)

    def q_tile(r0, rows):
        q = q_ref[pl.ds(r0, rows), :]
        s = _dot_nt(q, keys(0, ROW_OFF))
        col = lax.broadcasted_iota(jnp.int32, (rows, ROW_OFF), 1)
        s = jnp.where(col >= N_PAD, s, NEG_BIG)
        m = jnp.max(s, axis=1, keepdims=True)
        acc = _dot(jnp.exp2(s - m).astype(BF16), _v_ones(v_ref, 0, ROW_OFF))
        for c in range(seq // tk):
            k0 = ROW_OFF + c * tk
            s = _dot_nt(q, keys(k0, tk))
            m_new = jnp.maximum(m, jnp.max(s, axis=1, keepdims=True))
            p = jnp.exp2(s - m_new)
            acc = jnp.exp2(m - m_new) * acc + _dot(p.astype(BF16), _v_ones(v_ref, k0, tk))
            m = m_new
        o_ref[pl.ds(r0, rows), :] = (acc[:, :dv] / acc[:, dv:]).astype(BF16)

    _query_tiles(q_tile, o_ref, seq, tq)


def _mla_attn(q, kn, kr, v, seq):
    b, lp, _ = q.shape
    tq = _pick_tile(seq, 4096)
    tk = _pick_tile(seq, 256)
    head = lambda w: pl.BlockSpec((None, lp, w), lambda i, j: (i, 0, j))
    return pl.pallas_call(
        functools.partial(_mla_attn_kernel, seq=seq, tq=tq, tk=tk),
        grid=(b, MLA_HEADS),
        in_specs=[head(MLA_QK), head(MLA_NOPE), pl.BlockSpec((None, lp, 128), lambda i, j: (i, 0, 0)),
                  head(MLA_V)],
        out_specs=head(MLA_V),
        out_shape=jax.ShapeDtypeStruct((b, lp, MLA_HEADS * MLA_V), BF16),
        compiler_params=_cparams(2),
        name="mla_attn",
    )(q, kn, kr, v)


def _diff_prep_kernel(h_ref, g_ref, w_ref, q_ref, k_ref, v_ref, *, q_scale):
    u = _rms(h_ref[...], g_ref[...]).astype(BF16)
    n = q_ref.shape[1]
    for c in range(n // 256):
        sl = slice(c * 256, (c + 1) * 256)
        q_ref[:, sl] = (_dot(u, w_ref[:, c * 256:(c + 1) * 256]) * q_scale).astype(BF16)
        k_ref[:, sl] = _dot(u, w_ref[:, n + c * 256:n + (c + 1) * 256]).astype(BF16)
        v_ref[:, sl] = _dot(u, w_ref[:, 2 * n + c * 256:2 * n + (c + 1) * 256]).astype(BF16)


def _diff_prep(h, g, w, j):
    rows = h.shape[0]
    tm = PREP_TILE if rows % PREP_TILE == 0 else ROW_TILE
    n = w.shape[2] // 3
    row = lambda i: (i, 0)
    return pl.pallas_call(
        functools.partial(_diff_prep_kernel, q_scale=DIFF_HEAD_DIM ** -0.5 * LOG2E),
        grid=(rows // tm,),
        in_specs=[pl.BlockSpec((tm, D_MODEL), row), _full(g.shape), _layer(w, j)],
        out_specs=[pl.BlockSpec((tm, n), row)] * 3,
        out_shape=[jax.ShapeDtypeStruct((rows, n), BF16)] * 3,
        compiler_params=_cparams(1),
        name="diff_prep",
    )(h, g, w)


BIAS_SPLIT = 32


def _split3(x):
    c1 = x.astype(BF16).astype(F32)
    c2 = (x - c1).astype(BF16).astype(F32)
    c3 = (x - c1 - c2).astype(BF16).astype(F32)
    return c1, c2, c3


def _diff_attn_kernel(slope_ref, q_ref, k_ref, v_ref, lam_ref, g_ref, o_ref, *,
                      seq, tq, tk, lambda_init):
    slope2 = slope_ref[pl.program_id(1)] * LOG2E
    lp4 = lam_ref[...]
    lam = (jnp.exp(jnp.sum(lp4[0:1] * lp4[1:2], axis=1, keepdims=True))
           - jnp.exp(jnp.sum(lp4[2:3] * lp4[3:4], axis=1, keepdims=True)) + lambda_init)
    g = g_ref[...]
    dv = v_ref.shape[1]
    n_chunks = seq // tk
    per_tile = tq // tk

    def lanes3(lane, x1, x2, x3):
        return jnp.where(lane % 3 == 0, x1, jnp.where(lane % 3 == 1, x2, x3))

    c1, c2, c3 = _split3(jnp.full((1, 128), slope2, F32))
    lane_k = lax.broadcasted_iota(jnp.int32, (tk, 128), 1)
    jj = lax.broadcasted_iota(jnp.int32, (tk, 128), 0).astype(F32)
    piece = lanes3(lane_k, c1, c2, c3)
    k_const = jnp.where(lane_k < 3, jj,
                        jnp.where(lane_k < 6, float(BIAS_SPLIT) * piece,
                                  jnp.where(lane_k < 9, piece, 0.0))).astype(BF16)

    def k_aug(k0, n, const):
        return jnp.concatenate([k_ref[pl.ds(k0, n), :], const], axis=1)

    def bias_rows(rows, first):
        ii = lax.broadcasted_iota(jnp.int32, (rows, 128), 0)
        if first:
            ii = ii - N_META
        lane_q = lax.broadcasted_iota(jnp.int32, (rows, 128), 1)
        in_a = (lane_q >= 3) & (lane_q < 6)
        base = jnp.where(lane_q < 3, lanes3(lane_q, c1, c2, c3),
                         jnp.where(in_a, -(ii >> 5).astype(F32),
                                   jnp.where(lane_q < 9, -(ii & (BIAS_SPLIT - 1)).astype(F32), 0.0))
                         ).astype(BF16)
        return base, jnp.where(in_a, -1.0, 0.0).astype(BF16)

    def above_diag(rows, width, off, first):
        d = (lax.broadcasted_iota(jnp.int32, (rows, width), 1)
             - lax.broadcasted_iota(jnp.int32, (rows, width), 0))
        if first:
            d = d + N_META
        return jnp.maximum((d - off).astype(F32), 0.0) * (2.0 * slope2)

    def q_tile(r0, rows, t, base, unit, fixes):
        first = isinstance(t, int)
        q = q_ref[pl.ds(r0, rows), :]
        lane = lax.broadcasted_iota(jnp.int32, q.shape, 1)
        zero = jnp.zeros_like(q)
        qs = (jnp.where(lane < DIFF_HEAD_DIM, q, zero), jnp.where(lane >= DIFF_HEAD_DIM, q, zero))

        def q_aug(sgn, n_hi):
            tt = base + unit * (n_hi if isinstance(n_hi, float) else n_hi.astype(BF16))
            if not isinstance(sgn, float):
                tt = tt * sgn.astype(BF16)
            elif sgn != 1.0:
                tt = tt * sgn
            tt = tt.astype(BF16)
            return [jnp.concatenate([qs[c], tt], axis=1) for c in range(2)]

        base_row = ROW_OFF + t * tq
        n_meta = float(base_row // BIAS_SPLIT) if first else (base_row // BIAS_SPLIT).astype(F32)
        qa = q_aug(1.0, n_meta)
        kk = k_aug(0, ROW_OFF, k_const[0:ROW_OFF])
        col = lax.broadcasted_iota(jnp.int32, (rows, ROW_OFF), 1)
        vv = _v_ones(v_ref, 0, ROW_OFF)
        fix = above_diag(rows, ROW_OFF, ROW_OFF, True) if first else None
        m, acc = [], []
        for c in range(2):
            s = _dot_nt(qa[c], kk)
            if first:
                s = s - fix
            s = jnp.where(col >= N_PAD, s, NEG_BIG)
            mc = jnp.max(s, axis=1, keepdims=True)
            m.append(mc)
            acc.append(_dot(jnp.exp2(s - mc).astype(BF16), vv))

        for rel in range(n_chunks):
            diag = rel < per_tile
            if first:
                sgn = 1.0 if diag else -1.0
                n_hi = float(-(rel * tk) // BIAS_SPLIT)
                k0 = ROW_OFF + rel * tk
            else:
                raw = t * per_tile + rel
                wrapped = raw >= n_chunks
                c_idx = jnp.where(wrapped, raw - n_chunks, raw)
                sgn = 1.0 if diag else jnp.where(wrapped, 1.0, -1.0)
                n_hi = ((t * per_tile - c_idx) * (tk // BIAS_SPLIT)).astype(F32)
                k0 = pl.multiple_of(ROW_OFF + c_idx * tk, 128)
            qa = q_aug(sgn, n_hi)
            kk = k_aug(k0, tk, k_const)
            vv = _v_ones(v_ref, k0, tk)
            fix = fixes[rel] if diag else None
            for c in range(2):
                s = _dot_nt(qa[c], kk)
                if diag:
                    s = s - fix
                m_new = jnp.maximum(m[c], jnp.max(s, axis=1, keepdims=True))
                p = jnp.exp2(s - m_new)
                acc[c] = jnp.exp2(m[c] - m_new) * acc[c] + _dot(p.astype(BF16), vv)
                m[c] = m_new
        o = acc[0][:, :dv] / acc[0][:, dv:] - lam * (acc[1][:, :dv] / acc[1][:, dv:])
        o = _rms(o, g) * (1.0 - lambda_init)
        o_ref[pl.ds(r0, rows), :] = o.astype(BF16)

    o_ref[0:N_PAD, :] = jnp.zeros((N_PAD, o_ref.shape[1]), o_ref.dtype)
    rows0 = N_META + tq
    q_tile(N_PAD, rows0, 0, *bias_rows(rows0, True),
           [above_diag(rows0, tk, -rel * tk, True) for rel in range(per_tile)])

    base, unit = bias_rows(tq, False)
    fixes = [above_diag(tq, tk, -rel * tk, False) for rel in range(per_tile)]

    def body(i, carry):
        q_tile(pl.multiple_of(ROW_OFF + i * tq, 128), tq, i, base, unit, fixes)
        return carry

    lax.fori_loop(1, seq // tq, body, 0, unroll=2)


def _diff_attn(slopes, q, k, v, lam_p, g, seq, lambda_init):
    b, lp, _ = q.shape
    tq = _pick_tile(seq, 512)
    tk = _pick_tile(seq, 256)
    assert tk % BIAS_SPLIT == 0 and tk <= 256 and (lp + tq) // BIAS_SPLIT < 256
    w = 2 * DIFF_HEAD_DIM
    blk = pl.BlockSpec((None, lp, w), lambda i, j: (i, 0, j))
    return pl.pallas_call(
        functools.partial(_diff_attn_kernel, seq=seq, tq=tq, tk=tk, lambda_init=lambda_init),
        grid=(b, DIFF_HEADS),
        in_specs=[pl.BlockSpec(memory_space=pltpu.SMEM), blk, blk, blk,
                  _full(lam_p.shape), _full(g.shape)],
        out_specs=blk,
        out_shape=jax.ShapeDtypeStruct((b, lp, DIFF_HEADS * w), BF16),
        compiler_params=_cparams(2),
        name="diff_attn",
    )(slopes, q, k, v, lam_p, g)


def _gelu_tanh(x):
    c = math.sqrt(2.0 / math.pi)
    return x * (0.5 * (1.0 + jnp.tanh(c * (x + 0.044715 * (x * x * x)))))


def _lru_prep_kernel(h_ref, g_ref, w_ref, xb_ref, gate_ref):
    u = _rms(h_ref[...], g_ref[...]).astype(BF16)
    for c in range(LRU_WIDTH // 256):
        sl = slice(c * 256, (c + 1) * 256)
        xb_ref[:, sl] = _dot(u, w_ref[:, c * 256:(c + 1) * 256])
        gate = _dot(u, w_ref[:, LRU_WIDTH + c * 256:LRU_WIDTH + (c + 1) * 256])
        gate_ref[:, sl] = _gelu_tanh(gate).astype(BF16)


def _lru_prep(h, g, w, j):
    rows = h.shape[0]
    tm = PREP_TILE if rows % PREP_TILE == 0 else ROW_TILE
    row = lambda i: (i, 0)
    return pl.pallas_call(
        _lru_prep_kernel,
        grid=(rows // tm,),
        in_specs=[pl.BlockSpec((tm, D_MODEL), row), _full(g.shape), _layer(w, j)],
        out_specs=[pl.BlockSpec((tm, LRU_WIDTH), row)] * 2,
        out_shape=[jax.ShapeDtypeStruct((rows, LRU_WIDTH), F32),
                   jax.ShapeDtypeStruct((rows, LRU_WIDTH), BF16)],
        compiler_params=_cparams(1),
        name="lru_prep",
    )(h, g, w)


def _softplus(x):
    return jnp.maximum(x, 0.0) + jnp.log(1.0 + jnp.exp(-jnp.abs(x)))


def _group_scan(a, u, carry, reverse):
    row = lax.broadcasted_iota(jnp.int32, a.shape, 0)
    for k in (1, 2, 4):
        shift = (8 - k) if reverse else k
        keep = (row < 8 - k) if reverse else (row >= k)
        a_s = pltpu.roll(a, shift, 0)
        u_s = pltpu.roll(u, shift, 0)
        u = jnp.where(keep, a * u_s + u, u)
        a = jnp.where(keep, a * a_s, a)
    return a * carry + u


def _odd_part(n):
    while n % 2 == 0:
        n //= 2
    return n


def _lru_scan_kernel(xa_ref, xh_ref, gate_ref, cw_ref, cb_ref, wg_ref, bg_ref, lam_ref, o_ref,
                     xc_ref, y_ref, yb_ref, *, lp, seg):
    ch = LRU_CHUNK
    width = 2 * HALF
    sc = 8 * seg
    n_chunks = lp // sc
    x_refs = (xa_ref, xh_ref)

    def strided(t0, j):
        return pl.ds(t0 + j, 8, stride=seg)

    row8 = lax.broadcasted_iota(jnp.int32, (8, HALF), 0)
    for c in range(n_chunks):
        t0 = c * sc
        for hh in range(2):
            lanes = slice(hh * HALF, (hh + 1) * HALF)
            cw = cw_ref[:, lanes]
            cb = cb_ref[:, lanes]
            xs = {j: x_refs[hh][strided(t0, j), :] for j in range(-2 if c else 0, seg + (1 if c < n_chunks - 1 else 0))}
            if c == 0:
                for j in (-2, -1):
                    xs[j] = jnp.where(row8 >= 1, pltpu.roll(xs[seg + j], 1, 0), 0.0)
            if c == n_chunks - 1:
                xs[seg] = jnp.where(row8 <= 6, pltpu.roll(xs[0], 7, 0), 0.0)
            for j in range(seg):
                acc = cb + cw[0:1] * xs[j - 2] + cw[1:2] * xs[j - 1] + cw[2:3] * xs[j] + cw[3:4] * xs[j + 1]
                if t0 + j < N_PAD:
                    acc = jnp.where(row8 * seg + (t0 + j) >= N_PAD, acc, 0.0)
                xc_ref[t0 + 8 * j:t0 + 8 * j + 8, lanes] = acc

    def gates(xc, d, rate):
        xcb = xc.astype(BF16)
        r = jax.nn.sigmoid(_dot(xcb, wg_ref[d, 0]) + bg_ref[2 * d:2 * d + 1, :])
        i = jax.nn.sigmoid(_dot(xcb, wg_ref[d, 1]) + bg_ref[2 * d + 1:2 * d + 2, :])
        a = jnp.exp2(r * rate)
        x = 1.0 - a * a
        u = jnp.where(x > 0.0, x * lax.rsqrt(x), 0.0) * (i * xc)
        return a, u

    def scan_chunk(a, u, carry, reverse):
        hs, ps = [None] * seg, [None] * seg
        h = p = None
        for j in (reversed(range(seg)) if reverse else range(seg)):
            aj, uj = a[8 * j:8 * j + 8], u[8 * j:8 * j + 8]
            h = uj if h is None else aj * h + uj
            p = aj if p is None else aj * p
            hs[j], ps[j] = h, p
        ends = _group_scan(p, h, carry, reverse)
        row = lax.broadcasted_iota(jnp.int32, ends.shape, 0)
        if reverse:
            entering = jnp.where(row == 7, carry, pltpu.roll(ends, 7, 0))
            carry = ends[0:1]
        else:
            entering = jnp.where(row == 0, carry, pltpu.roll(ends, 1, 0))
            carry = ends[7:8]
        return [hs[j] + ps[j] * entering for j in range(seg)], carry

    def rate(d):
        return _softplus(-lam_ref[d:d + 1, :]) * (-LRU_C * LOG2E)

    rates = (rate(0), rate(1))

    def put_perm(ref, t0, ys):
        for j in range(seg):
            ref[0, strided(t0, j), :] = ys[j][:, :HALF]
            ref[1, strided(t0, j), :] = ys[j][:, HALF:]

    def both(c, carries):
        cf, cbk = carries
        tf = pl.multiple_of(c * sc, 8)
        tb = pl.multiple_of((n_chunks - 1 - c) * sc, 8)
        af, uf = gates(xc_ref[pl.ds(tf, sc), :], 0, rates[0])
        ab, ub = gates(xc_ref[pl.ds(tb, sc), :], 1, rates[1])
        yf, cf = scan_chunk(af, uf, cf, False)
        yb, cbk = scan_chunk(ab, ub, cbk, True)
        put_perm(y_ref, tf, yf)
        put_perm(yb_ref, tb, yb)
        return cf, cbk

    zero = jnp.zeros((1, width), F32)
    lax.fori_loop(0, n_chunks, both, (zero, zero), unroll=LRU_UNROLL)

    def finish(c, carry):
        rows = pl.ds(pl.multiple_of(c * ch, 128), ch)
        y = jnp.concatenate([y_ref[0, rows, :] + yb_ref[0, rows, :],
                             y_ref[1, rows, :] + yb_ref[1, rows, :]], axis=1)
        o_ref[rows, :] = (y * gate_ref[rows, :].astype(F32)).astype(BF16)
        return carry

    lax.fori_loop(0, lp // ch, finish, 0)


def _lru_scan(xb, gate, cw, cb, wg, bg, lam):
    b, lp, _ = xb.shape
    w = LRU_BLOCK_W
    seg = _odd_part(lp // 8)
    blk = pl.BlockSpec((None, lp, w), lambda i, j: (i, 0, j))
    half = lambda k: pl.BlockSpec((None, lp, HALF), lambda i, j: (i, 0, 2 * j + k))
    col = lambda r: pl.BlockSpec((r, w), lambda i, j: (0, j))
    return pl.pallas_call(
        functools.partial(_lru_scan_kernel, lp=lp, seg=seg),
        grid=(b, LRU_BLOCKS),
        in_specs=[half(0), half(1), blk, col(4), col(1),
                  pl.BlockSpec((2, 2, None, w, w), lambda i, j: (0, 0, j, 0, 0)),
                  col(4), col(2)],
        out_specs=blk,
        out_shape=jax.ShapeDtypeStruct((b, lp, LRU_WIDTH), BF16),
        scratch_shapes=[pltpu.VMEM((lp, w), F32), pltpu.VMEM((2, lp, HALF), F32), pltpu.VMEM((2, lp, HALF), F32)],
        compiler_params=_cparams(2),
        name="lru_scan",
    )(xb, xb, gate, cw, cb, wg, bg, lam)


def _post_kernel(o_ref, h_ref, wo_ref, g_ref, win_ref, wout_ref, out_ref):
    g = g_ref[...]
    x = h_ref[...] + _rms(_dot(o_ref[...], wo_ref[...]), g[1:2])
    xn = _rms(x, g[2:3]).astype(BF16)
    gg = _dot(xn, win_ref[:, :FFN_HIDDEN])
    uu = _dot(xn, win_ref[:, FFN_HIDDEN:])
    a = (gg * jax.nn.sigmoid(gg) * uu).astype(BF16)
    out_ref[...] = x + _rms(_dot(a, wout_ref[...]), g[3:4])


def _post(o, h, w_o, j, g, w_in, w_out, i, final=None):
    rows = h.shape[0]
    once = dict(pipeline_mode=pl.Buffered(1))
    weights = [_layer(w_o, j, **once), _full(g.shape), _layer(w_in, i, **once), _layer(w_out, i, **once)]
    if final is None:
        tm = POST_TILE if rows % POST_TILE == 0 else ROW_TILE
        grid = (rows // tm,)
        row = lambda r: (r, 0)
        acts = [pl.BlockSpec((tm, o.shape[1]), row), pl.BlockSpec((tm, D_MODEL), row)]
        out_rows, out_spec = rows, pl.BlockSpec((tm, D_MODEL), row)
    else:
        b, lp, seq = final
        tm = _pick_tile(seq, POST_TILE)
        per = seq // tm
        grid = (b, per)
        start = lambda bi, t: (pl.multiple_of(bi * lp + ROW_OFF + t * tm, 128), 0)
        acts = [pl.BlockSpec((pl.Element(tm), pl.Element(o.shape[1])), start),
                pl.BlockSpec((pl.Element(tm), pl.Element(D_MODEL)), start)]
        out_rows, out_spec = b * seq, pl.BlockSpec((tm, D_MODEL), lambda bi, t: (bi * per + t, 0))
    return pl.pallas_call(
        _post_kernel,
        grid=grid,
        in_specs=acts + weights,
        out_specs=out_spec,
        out_shape=jax.ShapeDtypeStruct((out_rows, D_MODEL), F32),
        compiler_params=_cparams(len(grid)),
        name="post_ffn",
    )(o, h, w_o, g, w_in, w_out)


def _rope_rows(lp):
    pos = jnp.arange(lp, dtype=F32) - float(N_PAD)
    inv_freq = ROPE_THETA ** (-jnp.arange(0, MLA_ROPE, 2, dtype=F32) / MLA_ROPE)
    ang = pos[:, None] * inv_freq[None, :]
    c, s = jnp.cos(ang), jnp.sin(ang)
    z64 = jnp.zeros((lp, 64), F32)
    return jnp.concatenate([c, c, z64], axis=1), jnp.concatenate([s, s, z64], axis=1)


def _with_swapped(rope):
    half = rope.shape[-1] // 2
    return jnp.concatenate([rope, -rope[..., half:], rope[..., :half]], axis=-1)


def kernel(x, meta_tokens, norm_g, mla_w_in, mla_q_norm, mla_kv_norm, mla_w_uq, mla_w_ukv, mla_w_o,
           diff_w_in, diff_lambda, diff_subln, diff_w_o, lru_w_in, lru_conv_w, lru_conv_b,
           lru_w_gates, lru_b_gates, lru_lambda, lru_w_o, ffn_w_in, ffn_w_out):
    b, seq, d = x.shape
    depth = norm_g.shape[0]
    lp = seq + ROW_OFF
    assert d == D_MODEL and seq % 128 == 0 and lp % LRU_CHUNK == 0 and lp % ROW_TILE == 0

    from_x = depth > 0 and lp % 128 == 0 and 2 * ROW_OFF <= lp // 8 <= seq
    if from_x:
        h = None
        head = jnp.concatenate([jnp.zeros((N_PAD, d), x.dtype), meta_tokens.astype(x.dtype)], axis=0)
    else:
        meta = jnp.broadcast_to(meta_tokens[None].astype(x.dtype), (b, N_META, d))
        h = jnp.concatenate([jnp.zeros((b, N_PAD, d), x.dtype), meta, x], axis=1).reshape(b * lp, d)
    cos, sin = _rope_rows(lp)
    slopes = 2.0 ** (-8.0 * jnp.arange(1, DIFF_HEADS + 1, dtype=F32) / DIFF_HEADS)

    n_mla = mla_w_in.shape[0]
    lat = MLA_Q_RANK + MLA_KV_RANK
    m_in = jnp.concatenate([mla_w_in[..., :lat], _with_swapped(mla_w_in[..., lat:])], axis=-1).astype(BF16)
    uq = mla_w_uq.reshape(n_mla, MLA_Q_RANK, MLA_HEADS, MLA_NOPE + MLA_ROPE)
    uq = jnp.concatenate([uq[..., :MLA_NOPE], _with_swapped(uq[..., MLA_NOPE:])], axis=-1)
    m_uq = uq.reshape(n_mla, MLA_Q_RANK, MLA_HEADS * MLA_QK).astype(BF16)
    ukv = mla_w_ukv.reshape(n_mla, MLA_KV_RANK, MLA_HEADS, MLA_NOPE + MLA_V)
    m_uk = ukv[..., :MLA_NOPE].reshape(n_mla, MLA_KV_RANK, MLA_HEADS * MLA_NOPE).astype(BF16)
    m_uv = ukv[..., MLA_NOPE:].reshape(n_mla, MLA_KV_RANK, MLA_HEADS * MLA_V).astype(BF16)
    m_qg, m_kvg = mla_q_norm[:, None, :], mla_kv_norm[:, None, :]
    w_o_stacks = (mla_w_o.astype(BF16), diff_w_o.astype(BF16), lru_w_o.astype(BF16))
    d_in, l_in = diff_w_in.astype(BF16), lru_w_in.astype(BF16)
    f_in, f_out = ffn_w_in.astype(BF16), ffn_w_out.astype(BF16)

    for i in range(depth):
        kind, j = i % 3, i // 3
        g = norm_g[i]
        if kind == 0:
            if h is None:
                q, kn, kr, v, h = _mla_prep(None, g[0:1], j, m_in, m_qg, m_kvg, m_uq, m_uk, m_uv, cos, sin, lp,
                                            x_head=(x.reshape(b * seq, d), head, seq))
            else:
                q, kn, kr, v = _mla_prep(h, g[0:1], j, m_in, m_qg, m_kvg, m_uq, m_uk, m_uv, cos, sin, lp)
            o = _mla_attn(q.reshape(b, lp, -1), kn.reshape(b, lp, -1), kr.reshape(b, lp, -1),
                          v.reshape(b, lp, -1), seq)
        elif kind == 1:
            lambda_init = 0.8 - 0.6 * math.exp(-0.3 * i)
            q, k, v = _diff_prep(h, g[0:1], d_in, j)
            o = _diff_attn(slopes, q.reshape(b, lp, -1), k.reshape(b, lp, -1), v.reshape(b, lp, -1),
                           diff_lambda[j], diff_subln[j][None], seq, lambda_init)
        else:
            xb, gate = _lru_prep(h, g[0:1], l_in, j)
            o = _lru_scan(xb.reshape(b, lp, -1), gate.reshape(b, lp, -1), lru_conv_w[j],
                          lru_conv_b[j][None], lru_w_gates[j].astype(BF16),
                          lru_b_gates[j].reshape(4, LRU_WIDTH), lru_lambda[j])
        last = (b, lp, seq) if i == depth - 1 else None
        h = _post(o.reshape(b * lp, -1), h, w_o_stacks[kind], j, g, f_in, f_out, i, final=last)
    return h.reshape(b, seq, d)
```
